```python
import math
import jax, jax.numpy as jnp
from jax import lax
import numpy as np

D_MODEL = 1024
BATCH = 8
SEQ = 2048
DEPTH = 1
DEC_BATCH = 128
DEC_SEQ = 8
PAST_LEN = 16384
PAGE_SIZE = 128

SSD_EXPAND = 2
SSD_WIDTH = SSD_EXPAND * D_MODEL
SSD_HEAD_DIM = 64
SSD_HEADS = SSD_WIDTH // SSD_HEAD_DIM
SSD_GROUPS = 8
SSD_HEADS_PER_GROUP = SSD_HEADS // SSD_GROUPS
SSD_STATE = 128
SSD_CONV = 4
SSD_CHUNK = 128
SSD_CONV_DIM = SSD_WIDTH + 2 * SSD_GROUPS * SSD_STATE
S5_WIDTH = D_MODEL
S5_GROUP_SIZE = 16
S5_GROUPS = S5_WIDTH // S5_GROUP_SIZE
S5_STATE = 64
DT_MIN = 0.001
DT_MAX = 0.1
NORM_EPS = 1e-6
PROJ_SIZES = (SSD_WIDTH, SSD_CONV_DIM, SSD_HEADS, S5_WIDTH, S5_WIDTH, D_MODEL, D_MODEL)
PROJ_DIM = sum(PROJ_SIZES)

kernel_name = 'hybrid_ssd_s5_gated_decode_step'


def rmsnorm(x, w):
    xf = x.astype(jnp.float32)
    xf = xf * lax.rsqrt(jnp.mean(xf * xf, axis=-1, keepdims=True) + NORM_EPS)
    return (xf * w.astype(jnp.float32)).astype(x.dtype)


def causal_conv(xbc, conv_state, w, b):
    full = jnp.concatenate([conv_state.astype(xbc.dtype), xbc], axis=1)
    y = lax.conv_general_dilated(full, w.astype(xbc.dtype)[:, None, :], window_strides=(1,),
                                 padding='VALID', dimension_numbers=('NWC', 'WIO', 'NWC'),
                                 feature_group_count=xbc.shape[-1])
    return y + b.astype(xbc.dtype), full[:, -(SSD_CONV - 1):]


def ssd_chunked(x, dt, A, B, C, init_state):
    b, L = x.shape[0], x.shape[1]
    G, R, P, N = SSD_GROUPS, SSD_HEADS_PER_GROUP, SSD_HEAD_DIM, SSD_STATE
    Q = min(SSD_CHUNK, L)
    nc = -(-L // Q)
    pad = nc * Q - L
    X = x.astype(jnp.float32) * dt[..., None]
    dA = dt * A
    Bf = B.astype(jnp.float32)
    Cf = C.astype(jnp.float32)
    if pad:
        X = jnp.pad(X, ((0, 0), (0, pad), (0, 0), (0, 0)))
        dA = jnp.pad(dA, ((0, 0), (0, pad), (0, 0)))
        Bf = jnp.pad(Bf, ((0, 0), (0, pad), (0, 0), (0, 0)))
        Cf = jnp.pad(Cf, ((0, 0), (0, pad), (0, 0), (0, 0)))
    X = X.reshape(b, nc, Q, G, R, P)
    dA = dA.reshape(b, nc, Q, G, R).transpose(0, 3, 4, 1, 2)
    Bc = Bf.reshape(b, nc, Q, G, N)
    Cc = Cf.reshape(b, nc, Q, G, N)
    A_cs = jnp.cumsum(dA, axis=-1)
    causal = jnp.tril(jnp.ones((Q, Q), dtype=bool))
    Lmat = jnp.exp(jnp.where(causal, A_cs[..., :, None] - A_cs[..., None, :], -jnp.inf))
    CB = jnp.einsum('bclgn,bcsgn->bgcls', Cc, Bc)
    Y_diag = jnp.einsum('bgcls,bgrcls,bcsgrp->bclgrp', CB, Lmat, X)
    decay_states = jnp.exp(A_cs[..., -1:] - A_cs)
    states = jnp.einsum('bclgn,bgrcl,bclgrp->bcgrpn', Bc, decay_states, X)
    s0 = init_state.astype(jnp.float32).reshape(b, 1, G, R, P, N)
    states = jnp.concatenate([s0, states], axis=1)
    cz = jnp.cumsum(jnp.pad(A_cs[..., -1], ((0, 0), (0, 0), (0, 0), (1, 0))), axis=-1)
    causal_c = jnp.tril(jnp.ones((nc + 1, nc + 1), dtype=bool))
    decay_chunk = jnp.exp(jnp.where(causal_c, cz[..., :, None] - cz[..., None, :], -jnp.inf))
    new_states = jnp.einsum('bgrzc,bcgrpn->bzgrpn', decay_chunk, states)
    Y_off = jnp.einsum('bclgn,bcgrpn,bgrcl->bclgrp', Cc, new_states[:, :-1], jnp.exp(A_cs))
    Y = (Y_diag + Y_off).reshape(b, nc * Q, SSD_HEADS, P)[:, :L]
    return Y, new_states[:, -1].reshape(b, SSD_HEADS, P, N)


def s5_scan(u, lam_re, lam_im, log_dt, B_re, B_im, C_re, C_im, D, s0_re, s0_im):
    b, L = u.shape[0], u.shape[1]
    f32 = jnp.float32
    lam = lax.complex(lam_re.astype(f32), lam_im.astype(f32))
    step = jnp.exp(log_dt.astype(f32))[:, None]
    A_bar = jnp.exp(lam * step)
    B_bar = ((A_bar - 1.0) / lam)[..., None] * lax.complex(B_re.astype(f32), B_im.astype(f32))
    uf = u.astype(f32)
    Bu = jnp.einsum('gpk,blgk->blgp', B_bar, uf.reshape(b, L, S5_GROUPS, S5_GROUP_SIZE))
    s0 = lax.complex(s0_re.astype(f32), s0_im.astype(f32))
    Bu = Bu.at[:, 0].add(A_bar * s0)
    A_seq = jnp.broadcast_to(A_bar, Bu.shape)

    def combine(e1, e2):
        a1, b1 = e1
        a2, b2 = e2
        return a2 * a1, a2 * b1 + b2

    _, states = lax.associative_scan(combine, (A_seq, Bu), axis=1)
    Cmat = lax.complex(C_re.astype(f32), C_im.astype(f32))
    y = jnp.einsum('gkp,blgp->blgk', Cmat, states).real.reshape(b, L, S5_WIDTH) + D.astype(f32) * uf
    final = states[:, -1]
    return y.astype(u.dtype), final.real, final.imag


def hybrid_layer(x, conv0, ssm0, re0, im0, norm_w, w_in, conv_w, conv_b, dt_bias, A_log, D_ssd,
                 ssd_norm_w, w_down_ssd, lam_re, lam_im, log_dt, B_re, B_im, C_re, C_im, D_s5,
                 w_glu, b_glu, w_down_s5, w_out):
    b, L = x.shape[0], x.shape[1]
    h = rmsnorm(x, norm_w)
    proj = h @ w_in.astype(h.dtype)
    splits = [int(i) for i in np.cumsum(PROJ_SIZES)[:-1]]
    z_ssd, xbc, dt_raw, u5, z5, g_a, g_b = jnp.split(proj, splits, axis=-1)
    xbc, new_conv = causal_conv(xbc, conv0, conv_w, conv_b)
    xbc = jax.nn.silu(xbc)
    xs, Bs, Cs = jnp.split(xbc, [SSD_WIDTH, SSD_WIDTH + SSD_GROUPS * SSD_STATE], axis=-1)
    xs = xs.reshape(b, L, SSD_HEADS, SSD_HEAD_DIM)
    Bs = Bs.reshape(b, L, SSD_GROUPS, SSD_STATE)
    Cs = Cs.reshape(b, L, SSD_GROUPS, SSD_STATE)
    dt = jax.nn.softplus(dt_raw.astype(jnp.float32) + dt_bias.astype(jnp.float32))
    A = -jnp.exp(A_log.astype(jnp.float32))
    y_ssd, new_ssm = ssd_chunked(xs, dt, A, Bs, Cs, ssm0)
    y_ssd = y_ssd + D_ssd.astype(jnp.float32)[:, None] * xs.astype(jnp.float32)
    yz = (y_ssd.reshape(b, L, SSD_WIDTH) * jax.nn.silu(z_ssd.astype(jnp.float32)))
    yz = yz.reshape(b, L, SSD_GROUPS, SSD_WIDTH // SSD_GROUPS)
    yz = yz * lax.rsqrt(jnp.mean(yz * yz, axis=-1, keepdims=True) + NORM_EPS)
    yz = (yz.reshape(b, L, SSD_WIDTH) * ssd_norm_w.astype(jnp.float32)).astype(x.dtype)
    y_a = yz @ w_down_ssd.astype(x.dtype)
    y5, new_re, new_im = s5_scan(u5, lam_re, lam_im, log_dt, B_re, B_im, C_re, C_im, D_s5, re0, im0)
    y5 = jax.nn.gelu(y5)
    y5 = y5 * jax.nn.sigmoid(y5 @ w_glu.astype(y5.dtype) + b_glu.astype(y5.dtype))
    y5 = y5 * jax.nn.silu(z5)
    y_b = y5 @ w_down_s5.astype(x.dtype)
    mix = jax.nn.sigmoid(g_a) * y_a + jax.nn.sigmoid(g_b) * y_b
    out = x + mix @ w_out.astype(x.dtype)
    return out, new_conv, new_ssm, new_re, new_im


def setup_inputs(seed: int = 0) -> dict:
    key = jax.random.key(seed)
    ks = iter(jax.random.split(key, 40))
    f32 = jnp.float32

    def nrm(shape, scale):
        return jax.random.normal(next(ks), shape, f32) * scale

    def unif(shape, lo, hi):
        return jax.random.uniform(next(ks), shape, f32, lo, hi)

    dt0 = jnp.exp(unif((DEPTH, SSD_HEADS), math.log(DT_MIN), math.log(DT_MAX)))
    return {
        'x_prompt': nrm((BATCH, SEQ, D_MODEL), 1.0),
        'x_sample': nrm((DEC_BATCH, DEC_SEQ, D_MODEL), 1.0),
        'state_conv': nrm((DEPTH, DEC_BATCH, SSD_CONV - 1, SSD_CONV_DIM), 1.0),
        'state_ssm': nrm((DEPTH, DEC_BATCH, SSD_HEADS, SSD_HEAD_DIM, SSD_STATE), 0.5),
        'state_s5_re': nrm((DEPTH, DEC_BATCH, S5_GROUPS, S5_STATE), 0.3),
        'state_s5_im': nrm((DEPTH, DEC_BATCH, S5_GROUPS, S5_STATE), 0.3),
        'norm_w': 1.0 + nrm((DEPTH, D_MODEL), 0.02),
        'w_in': nrm((DEPTH, D_MODEL, PROJ_DIM), D_MODEL ** -0.5),
        'conv_w': nrm((DEPTH, SSD_CONV, SSD_CONV_DIM), SSD_CONV ** -0.5),
        'conv_b': nrm((DEPTH, SSD_CONV_DIM), 0.02),
        'dt_bias': dt0 + jnp.log(-jnp.expm1(-dt0)),
        'A_log': jnp.log(unif((DEPTH, SSD_HEADS), 1.0, 16.0)),
        'D_ssd': 1.0 + nrm((DEPTH, SSD_HEADS), 0.02),
        'ssd_norm_w': 1.0 + nrm((DEPTH, SSD_WIDTH), 0.02),
        'w_down_ssd': nrm((DEPTH, SSD_WIDTH, D_MODEL), SSD_WIDTH ** -0.5),
        'lam_re': -0.5 + nrm((DEPTH, S5_GROUPS, S5_STATE), 0.01),
        'lam_im': math.pi * jnp.arange(S5_STATE, dtype=f32) + nrm((DEPTH, S5_GROUPS, S5_STATE), 0.01),
        'log_dt': unif((DEPTH, S5_GROUPS), math.log(DT_MIN), math.log(DT_MAX)),
        'B_re': nrm((DEPTH, S5_GROUPS, S5_STATE, S5_GROUP_SIZE), (2 * S5_GROUP_SIZE) ** -0.5),
        'B_im': nrm((DEPTH, S5_GROUPS, S5_STATE, S5_GROUP_SIZE), (2 * S5_GROUP_SIZE) ** -0.5),
        'C_re': nrm((DEPTH, S5_GROUPS, S5_GROUP_SIZE, S5_STATE), S5_STATE ** -0.5),
        'C_im': nrm((DEPTH, S5_GROUPS, S5_GROUP_SIZE, S5_STATE), S5_STATE ** -0.5),
        'D_s5': nrm((DEPTH, S5_WIDTH), 1.0),
        'w_glu': nrm((DEPTH, S5_WIDTH, S5_WIDTH), S5_WIDTH ** -0.5),
        'b_glu': nrm((DEPTH, S5_WIDTH), 0.02),
        'w_down_s5': nrm((DEPTH, S5_WIDTH, D_MODEL), S5_WIDTH ** -0.5),
        'w_out': nrm((DEPTH, D_MODEL, D_MODEL), D_MODEL ** -0.5),
        'final_norm_w': 1.0 + nrm((D_MODEL,), 0.02),
    }


def reference(x_prompt, x_sample, state_conv, state_ssm, state_s5_re, state_s5_im, norm_w, w_in,
              conv_w, conv_b, dt_bias, A_log, D_ssd, ssd_norm_w, w_down_ssd, lam_re, lam_im, log_dt,
              B_re, B_im, C_re, C_im, D_s5, w_glu, b_glu, w_down_s5, w_out, final_norm_w):
    def run(x, conv0, ssm0, re0, im0):
        convs, ssms, res, ims = [], [], [], []
        for l in range(DEPTH):
            x, c, s, r, i = hybrid_layer(
                x, conv0[l], ssm0[l], re0[l], im0[l], norm_w[l], w_in[l], conv_w[l], conv_b[l],
                dt_bias[l], A_log[l], D_ssd[l], ssd_norm_w[l], w_down_ssd[l], lam_re[l], lam_im[l],
                log_dt[l], B_re[l], B_im[l], C_re[l], C_im[l], D_s5[l], w_glu[l], b_glu[l],
                w_down_s5[l], w_out[l])
            convs.append(c)
            ssms.append(s)
            res.append(r)
            ims.append(i)
        return (rmsnorm(x, final_norm_w), jnp.stack(convs), jnp.stack(ssms),
                jnp.stack(res), jnp.stack(ims))

    bp = x_prompt.shape[0]
    zc = jnp.zeros((DEPTH, bp, SSD_CONV - 1, SSD_CONV_DIM), x_prompt.dtype)
    zs = jnp.zeros((DEPTH, bp, SSD_HEADS, SSD_HEAD_DIM, SSD_STATE), jnp.float32)
    z5 = jnp.zeros((DEPTH, bp, S5_GROUPS, S5_STATE), jnp.float32)
    y_prompt, conv_p, ssm_p, re_p, im_p = run(x_prompt, zc, zs, z5, z5)
    y_sample, conv_s, ssm_s, re_s, im_s = run(x_sample, state_conv, state_ssm, state_s5_re, state_s5_im)
    return (y_prompt, y_sample, conv_p, ssm_p, re_p, im_p, conv_s, ssm_s, re_s, im_s)
```

```python
import functools

import jax
import jax.numpy as jnp
from jax import lax
from jax.experimental import pallas as pl
from jax.experimental.pallas import tpu as pltpu

F32 = jnp.float32
BF16 = jnp.bfloat16

D_MODEL = 1024
SSD_WIDTH = 2048
SSD_HEAD_DIM = 64
SSD_HEADS = 32
SSD_GROUPS = 8
SSD_STATE = 128
SSD_CONV = 4
SSD_CHUNK = 128
SSD_CONV_DIM = 4096
S5_WIDTH = 1024
S5_GROUP_SIZE = 16
S5_GROUPS = 64
S5_STATE = 64
NORM_EPS = 1e-6

LANES = 128
SUBLANES = 8

PROJ_COLS = 10240 + LANES
COL_TILE = 1152
ROW_TILE = 1024
CB_B, CB_C, CB_U5, CB_Z5, CB_GA, CB_GB = 4, 5, 6, 7, 8, 9
CB_DT = 10240 // LANES

VMEM_LIMIT = 56 * 1024 * 1024


def _cparams(sem):
    return pltpu.CompilerParams(dimension_semantics=sem, vmem_limit_bytes=VMEM_LIMIT)


def _sigmoid(x):
    return jax.nn.sigmoid(x)


def _silu(x):
    return x * jax.nn.sigmoid(x)


def _split3(a):
    hi = a.astype(BF16)
    r = a - hi.astype(F32)
    mid = r.astype(BF16)
    lo = (r - mid.astype(F32)).astype(BF16)
    return hi, mid, lo


def _dot(a, b):
    return jnp.dot(a, b, preferred_element_type=F32)


def _dot_nt(a, b):
    return lax.dot_general(a, b, (((1,), (1,)), ((), ())), preferred_element_type=F32)


def _dot_hi(a, b):
    a_hi = a.astype(BF16)
    a_lo = (a - a_hi.astype(F32)).astype(BF16)
    b_hi = b.astype(BF16)
    b_lo = (b - b_hi.astype(F32)).astype(BF16)
    return _dot(a_hi, b_hi) + _dot(a_hi, b_lo) + _dot(a_lo, b_hi)


def _s5_prep_body(lr_ref, li_ref, ldt_ref, bre_ref, bim_ref, cre_ref, cim_ref,
                  kmat_ref, car_ref, cain_ref, binr_ref, bini_ref, aqr_ref, aqi_ref,
                  pr_ref, pi_ref):
    j = pl.program_id(0)

    @pl.when(j == 0)
    def _():
        pr_ref[...] = jnp.ones_like(pr_ref)
        pi_ref[...] = jnp.zeros_like(pi_ref)

    lr = lr_ref[...]
    li = li_ref[...]
    step = jnp.exp(ldt_ref[...])
    mag = jnp.exp(lr * step)
    ar = mag * jnp.cos(li * step)
    ai = mag * jnp.sin(li * step)
    den = lr * lr + li * li
    nr = ar - 1.0
    cr = (nr * lr + ai * li) / den
    ci = (ai * lr - nr * li) / den
    bbr = cr * bre_ref[...] - ci * bim_ref[...]
    bbi = cr * bim_ref[...] + ci * bre_ref[...]

    pr = pr_ref[...]
    pi = pi_ref[...]
    car = cre_ref[...] * pr - cim_ref[...] * pi
    cai = cre_ref[...] * pi + cim_ref[...] * pr
    car_ref[...] = car
    cain_ref[...] = -cai
    binr_ref[...] = pr * bbr - pi * bbi
    bini_ref[...] = pr * bbi + pi * bbr
    aqr_ref[...] = pr
    aqi_ref[...] = pi

    n = S5_GROUPS * S5_STATE
    seg = jnp.where(jnp.right_shift(lax.broadcasted_iota(jnp.int32, (n, LANES), 0), 6)
                    == lax.broadcasted_iota(jnp.int32, (n, LANES), 1), 1.0, 0.0).astype(BF16)
    for kp in range(S5_GROUP_SIZE):
        prod = car[kp:kp + 1, :] * bbr - cai[kp:kp + 1, :] * bbi
        hi, mid, lo = _split3(prod)
        kmat_ref[kp * S5_GROUP_SIZE:(kp + 1) * S5_GROUP_SIZE, :] = (
            _dot(hi, seg) + _dot(mid, seg) + _dot(lo, seg))

    pr_ref[...] = pr * ar - pi * ai
    pi_ref[...] = pr * ai + pi * ar


def _s5_prep(q, lam_re, lam_im, log_dt, b_re, b_im, c_re, c_im):
    g, p, k = S5_GROUPS, S5_STATE, S5_GROUP_SIZE
    n = g * p
    row = lambda a: a.reshape(1, n)
    bt = lambda a: jnp.transpose(a, (2, 0, 1)).reshape(k, n)
    ct = lambda a: jnp.transpose(a, (1, 0, 2)).reshape(k, n)
    ldt = jnp.broadcast_to(log_dt[:, None], (g, p)).reshape(1, n)
    full = lambda r: pl.BlockSpec((r, n), lambda j: (0, 0))
    step3 = lambda r, c, f: pl.BlockSpec((None, r, c), f)
    nj = q + 1
    kmat, car, cain, binr, bini, aqr, aqi = pl.pallas_call(
        _s5_prep_body,
        grid=(nj,),
        in_specs=[full(1), full(1), full(1), full(k), full(k), full(k), full(k)],
        out_specs=[step3(k * k, LANES, lambda j: (j, 0, 0)),
                   step3(k, n, lambda j: (j, 0, 0)),
                   step3(k, n, lambda j: (j, 0, 0)),
                   step3(k, n, lambda j: (q - j, 0, 0)),
                   step3(k, n, lambda j: (q - j, 0, 0)),
                   full(1), full(1)],
        out_shape=[jax.ShapeDtypeStruct((nj, k * k, LANES), F32),
                   jax.ShapeDtypeStruct((nj, k, n), F32),
                   jax.ShapeDtypeStruct((nj, k, n), F32),
                   jax.ShapeDtypeStruct((nj, k, n), F32),
                   jax.ShapeDtypeStruct((nj, k, n), F32),
                   jax.ShapeDtypeStruct((1, n), F32),
                   jax.ShapeDtypeStruct((1, n), F32)],
        scratch_shapes=[pltpu.VMEM((1, n), F32), pltpu.VMEM((1, n), F32)],
        compiler_params=_cparams(("arbitrary",)),
        name="s5_prep",
    )(row(lam_re), row(lam_im), ldt, bt(b_re), bt(b_im), ct(c_re), ct(c_im))

    kmat = kmat[:q, :, :g].reshape(q, k, k, g)
    kmat = jnp.transpose(kmat, (3, 0, 1, 2))
    tt = jnp.arange(q)
    lag = tt[:, None] - tt[None, :]
    toep = jnp.where((lag >= 0)[None, :, :, None, None],
                     kmat[:, jnp.clip(lag, 0, q - 1)], 0.0)
    toep = jnp.transpose(toep, (0, 1, 3, 2, 4)).reshape(g, q * k, q * k)
    cr_ = jnp.transpose(car[1:].reshape(q, k, g, p), (2, 0, 1, 3)).reshape(g, q * k, p)
    ci_ = jnp.transpose(cain[1:].reshape(q, k, g, p), (2, 0, 1, 3)).reshape(g, q * k, p)
    coff = jnp.concatenate([cr_, ci_], axis=-1)
    br_ = jnp.transpose(binr[1:].reshape(q, k, g, p), (2, 3, 0, 1)).reshape(g, p, q * k)
    bi_ = jnp.transpose(bini[1:].reshape(q, k, g, p), (2, 3, 0, 1)).reshape(g, p, q * k)
    bin_ = jnp.concatenate([br_, bi_], axis=1)
    return toep, bin_, coff, aqr.reshape(g, p), aqi.reshape(g, p)


def _in_proj_body(x_ref, nw_ref, w_ref, o_ref, h_ref):
    @pl.when(pl.program_id(1) == 0)
    def _():
        x = x_ref[...]
        ms = jnp.mean(x * x, axis=-1, keepdims=True)
        h_ref[...] = (x * lax.rsqrt(ms + NORM_EPS) * nw_ref[...]).astype(BF16)

    o_ref[...] = _dot(h_ref[...], w_ref[...])


def _in_proj(x2d, norm_w, w_cat):
    m = x2d.shape[0]
    return pl.pallas_call(
        _in_proj_body,
        grid=(m // ROW_TILE, PROJ_COLS // COL_TILE),
        in_specs=[pl.BlockSpec((ROW_TILE, D_MODEL), lambda i, j: (i, 0)),
                  pl.BlockSpec((1, D_MODEL), lambda i, j: (0, 0)),
                  pl.BlockSpec((D_MODEL, COL_TILE), lambda i, j: (0, j))],
        out_specs=pl.BlockSpec((ROW_TILE, COL_TILE), lambda i, j: (i, j)),
        out_shape=jax.ShapeDtypeStruct((m, PROJ_COLS), F32),
        scratch_shapes=[pltpu.VMEM((ROW_TILE, D_MODEL), BF16)],
        compiler_params=_cparams(("arbitrary", "arbitrary")),
        name="in_proj",
    )(x2d, norm_w, w_cat)


CONV_PAD = SUBLANES


def _cumsum_rows(x):
    row = lax.broadcasted_iota(jnp.int32, x.shape, 0)
    s = 1
    while s < x.shape[0]:
        x = x + jnp.where(row >= s, pltpu.roll(x, s, axis=0), 0.0)
        s *= 2
    return x


def _ssd_body(z_ref, xs_ref, b_ref, c_ref, dt_ref, cw_ref, cb_ref, dtb_ref, alog_ref,
              dexp_ref, nw_ref, conv0_ref, ssm0_ref,
              yz_ref, convo_ref, ssmo_ref,
              cbuf, act, xt, zt, yt, state, *, lr):
    q = SSD_CHUNK
    c = pl.program_id(1)
    k1 = SSD_CONV - 1
    base = CONV_PAD

    @pl.when(c == 0)
    def _():
        state[...] = ssm0_ref[...]
        cbuf[base - k1:base, :] = conv0_ref[...]
        if lr < q:
            cbuf[base + lr:, :] = jnp.zeros((q - lr, SSD_CONV_DIM), F32)

    cbuf[base:base + lr, 0:SSD_WIDTH] = xs_ref[...]
    cbuf[base:base + lr, SSD_WIDTH:SSD_WIDTH + 1024] = b_ref[...]
    cbuf[base:base + lr, SSD_WIDTH + 1024:] = c_ref[...]

    sec = 512
    for s0 in range(0, SSD_CONV_DIM, sec):
        acc = cb_ref[:, s0:s0 + sec] + cw_ref[0:1, s0:s0 + sec] * cbuf[base - k1:base - k1 + q, s0:s0 + sec]
        for j in range(1, SSD_CONV):
            acc = acc + cw_ref[j:j + 1, s0:s0 + sec] * cbuf[base - k1 + j:base - k1 + j + q, s0:s0 + sec]
        act[:, s0:s0 + sec] = _silu(acc)

    tail = cbuf[base + lr - k1:base + lr, :]
    convo_ref[...] = tail
    cbuf[base - k1:base, :] = tail

    if lr < q:
        dt_raw = jnp.concatenate([dt_ref[...], jnp.zeros((q - lr, LANES), F32)], axis=0)
    else:
        dt_raw = dt_ref[...]
    rowi = lax.broadcasted_iota(jnp.int32, (q, LANES), 0)
    dt = jnp.where(rowi < lr, jax.nn.softplus(dt_raw + dtb_ref[...]), 0.0)
    a_cs = _cumsum_rows(dt * (-jnp.exp(alog_ref[...])))
    a_cst = a_cs.T
    dtt = dt.T

    for j in range(SSD_WIDTH // LANES):
        sl = slice(j * LANES, (j + 1) * LANES)
        xt[sl, :] = act[:, sl].T
    if lr < q:
        zfull = jnp.concatenate([z_ref[...], jnp.zeros((q - lr, SSD_WIDTH), F32)], axis=0)
    else:
        zfull = z_ref[...]
    for j in range(SSD_WIDTH // LANES):
        sl = slice(j * LANES, (j + 1) * LANES)
        zt[sl, :] = zfull[:, sl].T

    s_idx = lax.broadcasted_iota(jnp.int32, (q, q), 0)
    l_idx = lax.broadcasted_iota(jnp.int32, (q, q), 1)
    causal = l_idx >= s_idx
    hp = SSD_HEAD_DIM
    gw = SSD_WIDTH // SSD_GROUPS
    for g in range(SSD_GROUPS):
        bg = act[:, SSD_WIDTH + g * SSD_STATE:SSD_WIDTH + (g + 1) * SSD_STATE].astype(BF16)
        cg = act[:, SSD_WIDTH + 1024 + g * SSD_STATE:SSD_WIDTH + 1024 + (g + 1) * SSD_STATE].astype(BF16)
        cbt = _dot_nt(bg, cg)
        stg = state[g * gw:(g + 1) * gw, :]
        yoff = _dot_nt(stg.astype(BF16), cg)
        for r in range(SSD_HEADS // SSD_GROUPS):
            h = g * (SSD_HEADS // SSD_GROUPS) + r
            arow = a_cst[h:h + 1, :]
            acol = a_cs[:, h:h + 1]
            lt = jnp.where(causal, jnp.exp(jnp.minimum(arow - acol, 0.0)), 0.0)
            wt = (cbt * lt).astype(BF16)
            xh = xt[h * hp:(h + 1) * hp, :]
            xdt = xh * dtt[h:h + 1, :]
            yh = (_dot(xdt.astype(BF16), wt) + yoff[r * hp:(r + 1) * hp, :] * jnp.exp(arow)
                  + dexp_ref[h * hp:(h + 1) * hp, :] * xh)
            yt[h * hp:(h + 1) * hp, :] = yh
            atot = arow[:, q - 1:q]
            xdec = xdt * jnp.exp(atot - arow)
            state[h * hp:(h + 1) * hp, :] = (jnp.exp(atot) * stg[r * hp:(r + 1) * hp, :]
                                             + _dot(xdec.astype(BF16), bg))

    ssmo_ref[...] = state[...]

    for g in range(SSD_GROUPS):
        sl = slice(g * gw, (g + 1) * gw)
        v = yt[sl, :] * _silu(zt[sl, :])
        ms = jnp.mean(v * v, axis=0, keepdims=True)
        yt[sl, :] = v * lax.rsqrt(ms + NORM_EPS) * nw_ref[sl, :]
    for j in range(SSD_WIDTH // LANES):
        sl = slice(j * LANES, (j + 1) * LANES)
        yz_ref[:, sl] = yt[sl, :].T[0:lr, :]


def _ssd(proj, nseq, nchunk, lr, conv_w, conv_b, dt_bias, a_log, d_ssd, ssd_norm_w, conv0, ssm0):
    m = proj.shape[0]
    rb = lambda b, c: b * nchunk + c
    hrep = lambda v: jnp.broadcast_to(jnp.repeat(v, SSD_HEAD_DIM)[:, None], (SSD_WIDTH, LANES))
    padl = lambda v: jnp.pad(v, (0, LANES - SSD_HEADS)).reshape(1, LANES)
    const = lambda shape: pl.BlockSpec(shape, lambda b, c: (0, 0))
    body = functools.partial(_ssd_body, lr=lr)
    return pl.pallas_call(
        body,
        grid=(nseq, nchunk),
        in_specs=[pl.BlockSpec((lr, SSD_WIDTH), lambda b, c: (rb(b, c), 0)),
                  pl.BlockSpec((lr, SSD_WIDTH), lambda b, c: (rb(b, c), 1)),
                  pl.BlockSpec((lr, 1024), lambda b, c: (rb(b, c), CB_B)),
                  pl.BlockSpec((lr, 1024), lambda b, c: (rb(b, c), CB_C)),
                  pl.BlockSpec((lr, LANES), lambda b, c: (rb(b, c), CB_DT)),
                  const((SSD_CONV, SSD_CONV_DIM)),
                  const((1, SSD_CONV_DIM)),
                  const((1, LANES)),
                  const((1, LANES)),
                  const((SSD_WIDTH, LANES)),
                  const((SSD_WIDTH, LANES)),
                  pl.BlockSpec((None, SSD_CONV - 1, SSD_CONV_DIM), lambda b, c: (b, 0, 0)),
                  pl.BlockSpec((None, SSD_WIDTH, SSD_STATE), lambda b, c: (b, 0, 0))],
        out_specs=[pl.BlockSpec((lr, SSD_WIDTH), lambda b, c: (rb(b, c), 0)),
                   pl.BlockSpec((None, SSD_CONV - 1, SSD_CONV_DIM), lambda b, c: (b, 0, 0)),
                   pl.BlockSpec((None, SSD_WIDTH, SSD_STATE), lambda b, c: (b, 0, 0))],
        out_shape=[jax.ShapeDtypeStruct((m, SSD_WIDTH), F32),
                   jax.ShapeDtypeStruct((nseq, SSD_CONV - 1, SSD_CONV_DIM), F32),
                   jax.ShapeDtypeStruct((nseq, SSD_WIDTH, SSD_STATE), F32)],
        scratch_shapes=[pltpu.VMEM((CONV_PAD + SSD_CHUNK, SSD_CONV_DIM), F32),
                        pltpu.VMEM((SSD_CHUNK, SSD_CONV_DIM), F32),
                        pltpu.VMEM((SSD_WIDTH, LANES), F32),
                        pltpu.VMEM((SSD_WIDTH, LANES), F32),
                        pltpu.VMEM((SSD_WIDTH, LANES), F32),
                        pltpu.VMEM((SSD_WIDTH, SSD_STATE), F32)],
        compiler_params=_cparams(("arbitrary", "arbitrary")),
        name="ssd",
    )(proj, proj, proj, proj, proj, conv_w, conv_b.reshape(1, -1), padl(dt_bias), padl(a_log),
      hrep(d_ssd), jnp.broadcast_to(ssd_norm_w[:, None], (SSD_WIDTH, LANES)), conv0, ssm0)


S5_GB = 8


def _s5_body(*refs, q, scan):
    if scan:
        u_ref, toep_ref, bin_ref, coff_ref, aqr_ref, aqi_ref, y_ref, so_ref = refs
        s0_ref = None
    else:
        u_ref, toep_ref, bin_ref, coff_ref, aqr_ref, aqi_ref, s0_ref, y_ref, so_ref = refs
    k = S5_GROUP_SIZE
    p = S5_STATE
    lane = lax.broadcasted_iota(jnp.int32, (p, LANES), 1)
    for gl in range(S5_GB):
        z = u_ref[:, gl * k:(gl + 1) * k, :].reshape(q * k, LANES)
        zb = z.astype(BF16)
        aqr = aqr_ref[gl]
        aqi = aqi_ref[gl]
        if scan:
            bo = _dot(bin_ref[gl].astype(BF16), zb)
            xr, xi = bo[:p, :], bo[p:, :]
            pr, pi = aqr, aqi
            s = 1
            while s < LANES:
                m = lane >= s
                sr = jnp.where(m, pltpu.roll(xr, s, axis=1), 0.0)
                si = jnp.where(m, pltpu.roll(xi, s, axis=1), 0.0)
                xr, xi = xr + (pr * sr - pi * si), xi + (pr * si + pi * sr)
                pr, pi = pr * pr - pi * pi, 2.0 * pr * pi
                s *= 2
            so_ref[gl, :p, :] = xr[:, LANES - 1:LANES]
            so_ref[gl, p:, :] = xi[:, LANES - 1:LANES]
            m1 = lane >= 1
            sin_r = jnp.where(m1, pltpu.roll(xr, 1, axis=1), 0.0)
            sin_i = jnp.where(m1, pltpu.roll(xi, 1, axis=1), 0.0)
        else:
            bo = _dot_hi(bin_ref[gl], z)
            sin_r, sin_i = s0_ref[gl, :p, :], s0_ref[gl, p:, :]
            so_ref[gl, :p, :] = aqr * sin_r - aqi * sin_i + bo[:p, :]
            so_ref[gl, p:, :] = aqr * sin_i + aqi * sin_r + bo[p:, :]
        s_in = jnp.concatenate([sin_r, sin_i], axis=0).astype(BF16)
        y = _dot(toep_ref[gl], zb) + _dot(coff_ref[gl], s_in)
        y_ref[:, gl * k:(gl + 1) * k, :] = y.reshape(q, k, LANES)


def _s5(ut, q, scan, toep, bin_, coff, aqr, aqi, s0t=None):
    ctot = ut.shape[2]
    nlb = ctot // LANES
    ngb = S5_GROUPS // S5_GB
    qk = q * S5_GROUP_SIZE
    gspec = lambda r, c: pl.BlockSpec((S5_GB, r, c), lambda gb, lb: (gb, 0, 0))
    in_specs = [pl.BlockSpec((q, S5_GB * S5_GROUP_SIZE, LANES), lambda gb, lb: (0, gb, lb)),
                gspec(qk, qk), gspec(2 * S5_STATE, qk), gspec(qk, 2 * S5_STATE),
                gspec(S5_STATE, LANES), gspec(S5_STATE, LANES)]
    args = [ut, toep.astype(BF16), bin_, coff.astype(BF16),
            jnp.broadcast_to(aqr[:, :, None], (S5_GROUPS, S5_STATE, LANES)),
            jnp.broadcast_to(aqi[:, :, None], (S5_GROUPS, S5_STATE, LANES))]
    if scan:
        so_shape = jax.ShapeDtypeStruct((nlb, S5_GROUPS, 2 * S5_STATE, 1), F32)
        so_spec = pl.BlockSpec((None, S5_GB, 2 * S5_STATE, 1), lambda gb, lb: (lb, gb, 0, 0))
    else:
        in_specs.append(gspec(2 * S5_STATE, LANES))
        args.append(s0t)
        so_shape = jax.ShapeDtypeStruct((S5_GROUPS, 2 * S5_STATE, LANES), F32)
        so_spec = gspec(2 * S5_STATE, LANES)
    return pl.pallas_call(
        functools.partial(_s5_body, q=q, scan=scan),
        grid=(ngb, nlb),
        in_specs=in_specs,
        out_specs=[pl.BlockSpec((q, S5_GB * S5_GROUP_SIZE, LANES), lambda gb, lb: (0, gb, lb)), so_spec],
        out_shape=[jax.ShapeDtypeStruct(ut.shape, F32), so_shape],
        compiler_params=_cparams(("arbitrary", "arbitrary")),
        name="s5_scan" if scan else "s5_step",
    )(*args)


OUT_ROWS = 256


def _out_body(y5_ref, u5_ref, z5_ref, ga_ref, gb_ref, yz_ref, x_ref, d5_ref, wglu_ref, bglu_ref,
              wd5_ref, wda_ref, wout_ref, fnw_ref, o_ref):
    y5 = jax.nn.gelu(y5_ref[...] + d5_ref[...] * u5_ref[...])
    y5 = y5 * _sigmoid(_dot(y5.astype(BF16), wglu_ref[...]) + bglu_ref[...])
    y5 = y5 * _silu(z5_ref[...])
    y_b = _dot(y5.astype(BF16), wd5_ref[...])
    y_a = _dot(yz_ref[...].astype(BF16), wda_ref[...])
    mix = _sigmoid(ga_ref[...]) * y_a + _sigmoid(gb_ref[...]) * y_b
    out = x_ref[...] + _dot(mix.astype(BF16), wout_ref[...])
    ms = jnp.mean(out * out, axis=-1, keepdims=True)
    o_ref[...] = out * lax.rsqrt(ms + NORM_EPS) * fnw_ref[...]


def _out(y5, proj, yz, x2d, d_s5, w_glu, b_glu, w_down_s5, w_down_ssd, w_out, final_norm_w):
    m = x2d.shape[0]
    rowb = lambda w, cb: pl.BlockSpec((OUT_ROWS, w), lambda i: (i, cb))
    const = lambda r, c: pl.BlockSpec((r, c), lambda i: (0, 0))
    return pl.pallas_call(
        _out_body,
        grid=(m // OUT_ROWS,),
        in_specs=[rowb(1024, 0), rowb(1024, CB_U5), rowb(1024, CB_Z5), rowb(1024, CB_GA), rowb(1024, CB_GB),
                  rowb(SSD_WIDTH, 0), rowb(1024, 0),
                  const(1, 1024), const(1024, 1024), const(1, 1024), const(1024, 1024),
                  const(SSD_WIDTH, 1024), const(1024, 1024), const(1, 1024)],
        out_specs=rowb(1024, 0),
        out_shape=jax.ShapeDtypeStruct((m, D_MODEL), F32),
        compiler_params=_cparams(("arbitrary",)),
        name="out_stage",
    )(y5, proj, proj, proj, proj, yz, x2d, d_s5.reshape(1, -1), w_glu.astype(BF16), b_glu.reshape(1, -1),
      w_down_s5.astype(BF16), w_down_ssd.astype(BF16), w_out.astype(BF16), final_norm_w.reshape(1, -1))


def _run(x, conv0, ssm0, re0, im0, scan, prm):
    nseq, seqlen, _ = x.shape
    m = nseq * seqlen
    x2d = x.reshape(m, D_MODEL)
    proj = _in_proj(x2d, prm["norm_w"], prm["w_cat"])

    lr = min(SSD_CHUNK, seqlen)
    nchunk = seqlen // lr
    yz, conv_new, ssm_new = _ssd(proj, nseq, nchunk, lr, prm["conv_w"], prm["conv_b"], prm["dt_bias"],
                                 prm["a_log"], prm["d_ssd"], prm["ssd_norm_w"], conv0,
                                 ssm0.reshape(nseq, SSD_WIDTH, SSD_STATE))

    q = 16 if scan else seqlen
    toep, bin_, coff, aqr, aqi = prm["s5_q%d" % q]
    u5 = proj[:, CB_U5 * 1024:(CB_U5 + 1) * 1024]
    ut = jnp.transpose(u5.reshape(m // q, q, S5_WIDTH), (1, 2, 0))
    if scan:
        yt, so = _s5(ut, q, True, toep, bin_, coff, aqr, aqi)
        so = so.reshape(nseq, S5_GROUPS, 2, S5_STATE)
        re_new, im_new = so[:, :, 0, :], so[:, :, 1, :]
    else:
        s0t = jnp.transpose(jnp.concatenate([re0, im0], axis=-1), (1, 2, 0))
        yt, so = _s5(ut, q, False, toep, bin_, coff, aqr, aqi, s0t)
        so = jnp.transpose(so, (2, 0, 1))
        re_new, im_new = so[:, :, :S5_STATE], so[:, :, S5_STATE:]
    y5 = jnp.transpose(yt, (2, 0, 1)).reshape(m, S5_WIDTH)

    out = _out(y5, proj, yz, x2d, prm["d_s5"], prm["w_glu"], prm["b_glu"], prm["w_down_s5"],
               prm["w_down_ssd"], prm["w_out"], prm["final_norm_w"])
    return (out.reshape(nseq, seqlen, D_MODEL), conv_new[None],
            ssm_new.reshape(1, nseq, SSD_HEADS, SSD_HEAD_DIM, SSD_STATE), re_new[None], im_new[None])


def kernel(x_prompt, x_sample, state_conv, state_ssm, state_s5_re, state_s5_im, norm_w, w_in, conv_w, conv_b,
           dt_bias, A_log, D_ssd, ssd_norm_w, w_down_ssd, lam_re, lam_im, log_dt, B_re, B_im, C_re, C_im, D_s5,
           w_glu, b_glu, w_down_s5, w_out, final_norm_w):
    assert norm_w.shape[0] == 1, "single-layer trunk"
    w = w_in[0]
    o_dt = SSD_WIDTH + SSD_CONV_DIM
    w_cat = jnp.concatenate([w[:, :o_dt], w[:, o_dt + SSD_HEADS:], w[:, o_dt:o_dt + SSD_HEADS],
                             jnp.zeros((D_MODEL, LANES - SSD_HEADS), F32)], axis=1).astype(BF16)
    prm = dict(norm_w=norm_w[0].reshape(1, -1), w_cat=w_cat, conv_w=conv_w[0], conv_b=conv_b[0],
               dt_bias=dt_bias[0], a_log=A_log[0], d_ssd=D_ssd[0], ssd_norm_w=ssd_norm_w[0],
               w_down_ssd=w_down_ssd[0], d_s5=D_s5[0], w_glu=w_glu[0], b_glu=b_glu[0],
               w_down_s5=w_down_s5[0], w_out=w_out[0], final_norm_w=final_norm_w)
    s5p = (lam_re[0], lam_im[0], log_dt[0], B_re[0], B_im[0], C_re[0], C_im[0])
    prm["s5_q16"] = _s5_prep(16, *s5p)
    prm["s5_q8"] = _s5_prep(x_sample.shape[1], *s5p)

    bp = x_prompt.shape[0]
    zc = jnp.zeros((bp, SSD_CONV - 1, SSD_CONV_DIM), F32)
    zs = jnp.zeros((bp, SSD_HEADS, SSD_HEAD_DIM, SSD_STATE), F32)
    y_p, conv_p, ssm_p, re_p, im_p = _run(x_prompt, zc, zs, None, None, True, prm)
    y_s, conv_s, ssm_s, re_s, im_s = _run(x_sample, state_conv[0], state_ssm[0], state_s5_re[0],
                                          state_s5_im[0], False, prm)
    return (y_p, y_s, conv_p, ssm_p, re_p, im_p, conv_s, ssm_s, re_s, im_s)
```

```python
import functools

import jax
import jax.numpy as jnp
from jax import lax
from jax.experimental import pallas as pl
from jax.experimental.pallas import tpu as pltpu

F32 = jnp.float32
BF16 = jnp.bfloat16

D_MODEL = 1024
SSD_WIDTH = 2048
SSD_HEAD_DIM = 64
SSD_HEADS = 32
SSD_GROUPS = 8
SSD_HPG = SSD_HEADS // SSD_GROUPS
SSD_GW = SSD_WIDTH // SSD_GROUPS
SSD_STATE = 128
SSD_CONV = 4
SSD_CHUNK = 128
SSD_CONV_DIM = 4096
S5_WIDTH = 1024
S5_GROUP_SIZE = 16
S5_GROUPS = 64
S5_STATE = 64
NORM_EPS = 1e-6

LANES = 128
SUBLANES = 8

PROJ_COLS = 10240 + LANES
COL_TILE = 1152
ROW_TILE = 1024
CB_B, CB_C, CB_U5, CB_Z5, CB_GA, CB_GB = 4, 5, 6, 7, 8, 9
CB_DT = 10240 // LANES

VMEM_LIMIT = 56 * 1024 * 1024


def _cparams(sem):
    return pltpu.CompilerParams(dimension_semantics=sem, vmem_limit_bytes=VMEM_LIMIT)


def _sigmoid(x):
    return jax.nn.sigmoid(x)


def _silu(x):
    return x * jax.nn.sigmoid(x)


def _dot(a, b):
    return jnp.dot(a, b, preferred_element_type=F32)


def _dot_nt(a, b):
    return lax.dot_general(a, b, (((1,), (1,)), ((), ())), preferred_element_type=F32)


def _dot_hi(a, b):
    a_hi = a.astype(BF16)
    a_lo = (a - a_hi.astype(F32)).astype(BF16)
    b_hi = b.astype(BF16)
    b_lo = (b - b_hi.astype(F32)).astype(BF16)
    return _dot(a_hi, b_hi) + _dot(a_hi, b_lo) + _dot(a_lo, b_hi)


S5_Q = 16
PREP_GB = 8


def _zoh(lr, li, ldt):
    step = jnp.exp(ldt)
    mag = jnp.exp(lr * step)
    return mag * jnp.cos(li * step), mag * jnp.sin(li * step)


def _s5_prep_body(rowp_ref, lrc_ref, lic_ref, ldtc_ref, cre_ref, cim_ref, b1_ref, b2_ref,
                  toep_ref, coff_ref, bin_ref, binb_ref, aq16r_ref, aq16i_ref, aq8r_ref, aq8i_ref, ca_s):
    k, p, q = S5_GROUP_SIZE, S5_STATE, S5_Q
    qk = q * k
    lr, li = lrc_ref[...], lic_ref[...]
    ar, ai = _zoh(lr, li, ldtc_ref[...])
    den = lr * lr + li * li
    nr = ar - 1.0
    cr = (nr * lr + ai * li) / den
    ci = (ai * lr - nr * li) / den
    sgn_row = jnp.where(lax.broadcasted_iota(jnp.int32, (PREP_GB, 2 * p, 1), 1) < p, -1.0, 1.0)
    b1, b2 = b1_ref[...], b2_ref[...]
    bb = cr * b1 + sgn_row * ci * b2
    bb_sw = cr * b2 - sgn_row * ci * b1
    lane_t = jnp.right_shift(lax.broadcasted_iota(jnp.int32, (PREP_GB, 2 * p, qk), 2), 4)
    pr, pi = jnp.ones_like(ar), jnp.zeros_like(ar)
    pwr = jnp.zeros((PREP_GB, 2 * p, qk), F32)
    pwi = jnp.zeros((PREP_GB, 2 * p, qk), F32)
    for j in range(q + 1):
        if j < q:
            sel = lane_t == (q - 1 - j)
            pwr = jnp.where(sel, pr, pwr)
            pwi = jnp.where(sel, pi, pwi)
        if j == q // 2:
            aq8r_ref[...] = jnp.broadcast_to(pr[:, :p, :], aq8r_ref.shape)
            aq8i_ref[...] = jnp.broadcast_to(pi[:, :p, :], aq8i_ref.shape)
        if j == q:
            aq16r_ref[...] = jnp.broadcast_to(pr[:, :p, :], aq16r_ref.shape)
            aq16i_ref[...] = jnp.broadcast_to(pi[:, :p, :], aq16i_ref.shape)
        pr, pi = pr * ar - pi * ai, pr * ai + pi * ar
    binv = pwr * bb + sgn_row * pwi * bb_sw
    bin_ref[...] = binv
    binb_ref[...] = binv.astype(BF16)

    ar, ai = _zoh(rowp_ref[:, 0:1, :], rowp_ref[:, 1:2, :], rowp_ref[:, 2:3, :])
    low = lax.broadcasted_iota(jnp.int32, (PREP_GB, 1, 2 * p), 2) < p
    sgn_lane = jnp.where(low, 1.0, -1.0)
    cre, cim = cre_ref[...], cim_ref[...]
    pr, pi = jnp.ones_like(ar), jnp.zeros_like(ar)
    for j in range(q + 1):
        p1 = jnp.where(low, pr, pi)
        p2 = jnp.where(low, pi, pr)
        ca_s[:, j * k:(j + 1) * k, :] = sgn_lane * cre * p1 - cim * p2
        pr, pi = pr * ar - pi * ai, pr * ai + pi * ar
    coff_ref[...] = ca_s[:, k:, :].astype(BF16)

    lane_blk = jnp.right_shift(lax.broadcasted_iota(jnp.int32, (qk, qk), 1), 4)
    for g in range(PREP_GB):
        km = _dot_hi(ca_s[g, :qk, :], bb[g])
        tp = jnp.where(lane_blk == 0, km, 0.0)
        for t in range(1, q):
            sh = jnp.concatenate([jnp.zeros((t * k, qk), F32), km[:qk - t * k, :]], axis=0)
            tp = jnp.where(lane_blk == t, sh, tp)
        toep_ref[g] = tp.astype(BF16)


def _s5_prep(lam_re, lam_im, log_dt, b_re, b_im, c_re, c_im):
    g, p, k, q = S5_GROUPS, S5_STATE, S5_GROUP_SIZE, S5_Q
    qk = q * k
    dup = lambda a: jnp.concatenate([a, a], axis=-1)
    ldt = jnp.broadcast_to(log_dt[:, None], (g, 2 * p))
    rowp = jnp.stack([dup(lam_re), dup(lam_im), ldt], axis=1)
    b1 = jnp.tile(jnp.concatenate([b_re, b_im], axis=1), (1, 1, q))
    b2 = jnp.tile(jnp.concatenate([b_im, b_re], axis=1), (1, 1, q))
    blk = lambda r, c: pl.BlockSpec((PREP_GB, r, c), lambda i: (i, 0, 0))
    shp = lambda r, c, dt: jax.ShapeDtypeStruct((g, r, c), dt)
    toep, coff, bin_, binb, aq16r, aq16i, aq8r, aq8i = pl.pallas_call(
        _s5_prep_body,
        grid=(g // PREP_GB,),
        in_specs=[blk(3, 2 * p), blk(2 * p, 1), blk(2 * p, 1), blk(2 * p, 1),
                  blk(k, 2 * p), blk(k, 2 * p), blk(2 * p, qk), blk(2 * p, qk)],
        out_specs=[blk(qk, qk), blk(qk, 2 * p), blk(2 * p, qk), blk(2 * p, qk),
                   blk(p, LANES), blk(p, LANES), blk(p, LANES), blk(p, LANES)],
        out_shape=[shp(qk, qk, BF16), shp(qk, 2 * p, BF16), shp(2 * p, qk, F32), shp(2 * p, qk, BF16),
                   shp(p, LANES, F32), shp(p, LANES, F32), shp(p, LANES, F32), shp(p, LANES, F32)],
        scratch_shapes=[pltpu.VMEM((PREP_GB, (q + 1) * k, 2 * p), F32)],
        compiler_params=_cparams(("arbitrary",)),
        name="s5_prep",
    )(rowp, dup(lam_re)[:, :, None], dup(lam_im)[:, :, None], ldt[:, :, None],
      dup(c_re), dup(c_im), b1, b2)
    return dict(toep=toep, coff=coff, bin=bin_, binb=binb,
                aq={16: (aq16r, aq16i), 8: (aq8r, aq8i)})


def _in_proj_body(x_ref, nw_ref, w_ref, o_ref, h_ref):
    @pl.when(pl.program_id(1) == 0)
    def _():
        x = x_ref[...]
        ms = jnp.mean(x * x, axis=-1, keepdims=True)
        h_ref[...] = (x * lax.rsqrt(ms + NORM_EPS) * nw_ref[...]).astype(BF16)

    o_ref[...] = _dot(h_ref[...], w_ref[...])


def _in_proj(x2d, norm_w, w_cat):
    m = x2d.shape[0]
    return pl.pallas_call(
        _in_proj_body,
        grid=(m // ROW_TILE, PROJ_COLS // COL_TILE),
        in_specs=[pl.BlockSpec((ROW_TILE, D_MODEL), lambda i, j: (i, 0)),
                  pl.BlockSpec((1, D_MODEL), lambda i, j: (0, 0)),
                  pl.BlockSpec((D_MODEL, COL_TILE), lambda i, j: (0, j))],
        out_specs=pl.BlockSpec((ROW_TILE, COL_TILE), lambda i, j: (i, j)),
        out_shape=jax.ShapeDtypeStruct((m, PROJ_COLS), F32),
        scratch_shapes=[pltpu.VMEM((ROW_TILE, D_MODEL), BF16)],
        compiler_params=_cparams(("arbitrary", "arbitrary")),
        name="in_proj",
    )(x2d, norm_w, w_cat)


CONV_PAD = SUBLANES
CONV_SEC = 512


def _seg_cumsum_rows(x, seg):
    t = lax.broadcasted_iota(jnp.int32, x.shape, 0) & (seg - 1)
    s = 1
    while s < seg:
        x = x + jnp.where(t >= s, pltpu.roll(x, s, axis=0), 0.0)
        s *= 2
    return x


def _seg_revsum_rows(x, seg):
    n = x.shape[0]
    t = lax.broadcasted_iota(jnp.int32, x.shape, 0) & (seg - 1)
    s = 1
    while s < seg:
        x = x + jnp.where(t + s < seg, pltpu.roll(x, n - s, axis=0), 0.0)
        s *= 2
    return x


def _to_channel_major(src, dst):
    for j in range(SSD_WIDTH // LANES):
        sl = slice(j * LANES, (j + 1) * LANES)
        dst[sl, :] = src[:, sl].T


def _gate_norm_store(yt, zt, nw_ref, yz_ref):
    for g in range(SSD_GROUPS):
        sl = slice(g * SSD_GW, (g + 1) * SSD_GW)
        v = yt[sl, :] * _silu(zt[sl, :])
        ms = jnp.mean(v * v, axis=0, keepdims=True)
        yt[sl, :] = v * lax.rsqrt(ms + NORM_EPS) * nw_ref[sl, :]
    for j in range(SSD_WIDTH // LANES):
        sl = slice(j * LANES, (j + 1) * LANES)
        yz_ref[:, sl] = yt[sl, :].T


def _group_bc(act, g):
    bg = act[:, SSD_WIDTH + g * SSD_STATE:SSD_WIDTH + (g + 1) * SSD_STATE].astype(BF16)
    cg = act[:, SSD_WIDTH + 1024 + g * SSD_STATE:SSD_WIDTH + 1024 + (g + 1) * SSD_STATE].astype(BF16)
    return bg, cg


def _ssd_body(z_ref, xs_ref, b_ref, c_ref, dt_ref, cw_ref, cb_ref, dtb_ref, alog_ref, dexp_ref, nw_ref,
              yz_ref, ssmo_ref, cbuf, act, xt, zt, yt, state):
    q = SSD_CHUNK
    k1 = SSD_CONV - 1
    base = CONV_PAD
    hp = SSD_HEAD_DIM

    @pl.when(pl.program_id(1) == 0)
    def _():
        state[...] = jnp.zeros_like(state)
        cbuf[base - k1:base, :] = jnp.zeros((k1, SSD_CONV_DIM), F32)

    cbuf[base:, 0:SSD_WIDTH] = xs_ref[...]
    cbuf[base:, SSD_WIDTH:SSD_WIDTH + 1024] = b_ref[...]
    cbuf[base:, SSD_WIDTH + 1024:] = c_ref[...]
    for s0 in range(0, SSD_CONV_DIM, CONV_SEC):
        cs = slice(s0, s0 + CONV_SEC)
        acc = cb_ref[:, cs] + cw_ref[0:1, cs] * cbuf[base - k1:base - k1 + q, cs]
        for j in range(1, SSD_CONV):
            acc = acc + cw_ref[j:j + 1, cs] * cbuf[base - k1 + j:base - k1 + j + q, cs]
        act[:, cs] = _silu(acc)
    cbuf[base - k1:base, :] = cbuf[base + q - k1:base + q, :]

    dt = jax.nn.softplus(dt_ref[...] + dtb_ref[...])
    a_cs = _seg_cumsum_rows(dt * (-jnp.exp(alog_ref[...])), q)
    a_cst = a_cs.T
    dtt = dt.T
    _to_channel_major(act, xt)
    _to_channel_major(z_ref, zt)

    causal = lax.broadcasted_iota(jnp.int32, (q, q), 1) >= lax.broadcasted_iota(jnp.int32, (q, q), 0)
    for g in range(SSD_GROUPS):
        bg, cg = _group_bc(act, g)
        cbt = _dot_nt(bg, cg)
        stg = state[g * SSD_GW:(g + 1) * SSD_GW, :]
        yoff = _dot_nt(stg.astype(BF16), cg)
        for r in range(SSD_HPG):
            h = g * SSD_HPG + r
            hs = slice(h * hp, (h + 1) * hp)
            arow = a_cst[h:h + 1, :]
            acol = a_cs[:, h:h + 1]
            lt = jnp.where(causal, jnp.exp(jnp.minimum(arow - acol, 0.0)), 0.0)
            wt = (cbt * lt).astype(BF16)
            xh = xt[hs, :]
            xdt = xh * dtt[h:h + 1, :]
            yt[hs, :] = (_dot(xdt.astype(BF16), wt) + yoff[r * hp:(r + 1) * hp, :] * jnp.exp(arow)
                         + dexp_ref[hs, :] * xh)
            atot = arow[:, q - 1:q]
            xdec = xdt * jnp.exp(atot - arow)
            state[hs, :] = jnp.exp(atot) * stg[r * hp:(r + 1) * hp, :] + _dot(xdec.astype(BF16), bg)

    ssmo_ref[...] = state[...]
    _gate_norm_store(yt, zt, nw_ref, yz_ref)


def _ssd_consts(conv_w, conv_b, dt_bias, a_log, d_ssd, ssd_norm_w):
    hrep = jnp.broadcast_to(jnp.repeat(d_ssd, SSD_HEAD_DIM)[:, None], (SSD_WIDTH, LANES))
    padl = lambda v: jnp.pad(v, (0, LANES - SSD_HEADS)).reshape(1, LANES)
    return (conv_w, conv_b.reshape(1, -1), padl(dt_bias), padl(a_log), hrep,
            jnp.broadcast_to(ssd_norm_w[:, None], (SSD_WIDTH, LANES)))


def _ssd_const_specs(n_grid):
    zeros = (0,) * 2
    const = lambda shape: pl.BlockSpec(shape, lambda *_: zeros)
    return [const((SSD_CONV, SSD_CONV_DIM)), const((1, SSD_CONV_DIM)), const((1, LANES)), const((1, LANES)),
            const((SSD_WIDTH, LANES)), const((SSD_WIDTH, LANES))]


def _ssd_proj_specs(rows, rowf):
    return [pl.BlockSpec((rows, SSD_WIDTH), lambda *a: (rowf(*a), 0)),
            pl.BlockSpec((rows, SSD_WIDTH), lambda *a: (rowf(*a), 1)),
            pl.BlockSpec((rows, 1024), lambda *a: (rowf(*a), CB_B)),
            pl.BlockSpec((rows, 1024), lambda *a: (rowf(*a), CB_C)),
            pl.BlockSpec((rows, LANES), lambda *a: (rowf(*a), CB_DT))]


def _ssd(proj, nseq, nchunk, consts):
    m = proj.shape[0]
    rowf = lambda b, c: b * nchunk + c
    chan = pltpu.VMEM((SSD_WIDTH, LANES), F32)
    return pl.pallas_call(
        _ssd_body,
        grid=(nseq, nchunk),
        in_specs=_ssd_proj_specs(SSD_CHUNK, rowf) + _ssd_const_specs(2),
        out_specs=[pl.BlockSpec((SSD_CHUNK, SSD_WIDTH), lambda b, c: (rowf(b, c), 0)),
                   pl.BlockSpec((None, SSD_WIDTH, SSD_STATE), lambda b, c: (b, 0, 0))],
        out_shape=[jax.ShapeDtypeStruct((m, SSD_WIDTH), F32),
                   jax.ShapeDtypeStruct((nseq, SSD_WIDTH, SSD_STATE), F32)],
        scratch_shapes=[pltpu.VMEM((CONV_PAD + SSD_CHUNK, SSD_CONV_DIM), F32),
                        pltpu.VMEM((SSD_CHUNK, SSD_CONV_DIM), F32),
                        chan, chan, chan, pltpu.VMEM((SSD_WIDTH, SSD_STATE), F32)],
        compiler_params=_cparams(("arbitrary", "arbitrary")),
        name="ssd",
    )(proj, proj, proj, proj, proj, *consts)


SSD_PACK = 16
SSD_TOK = SSD_CHUNK // SSD_PACK
SSD_SUB = 4


def _ssd_decay_body(dt_ref, dtb_ref, alog_ref, o_ref):
    da = jax.nn.softplus(dt_ref[...] + dtb_ref[...]) * (-jnp.exp(alog_ref[...]))
    o_ref[...] = jnp.exp(jnp.sum(da.reshape(o_ref.shape[0], SSD_TOK, LANES), axis=1))


def _ssd_decay(proj, nseq, dtb, alog):
    const = lambda shape: pl.BlockSpec(shape, lambda i: (0, 0))
    return pl.pallas_call(
        _ssd_decay_body,
        grid=(1,),
        in_specs=[pl.BlockSpec((nseq * SSD_TOK, LANES), lambda i: (0, CB_DT)), const((1, LANES)), const((1, LANES))],
        out_specs=const((nseq, LANES)),
        out_shape=jax.ShapeDtypeStruct((nseq, LANES), F32),
        compiler_params=_cparams(("arbitrary",)),
        name="ssd_decay",
    )(proj, dtb, alog)


def _ssd_packed_body(dec_ref, z_ref, xs_ref, b_ref, c_ref, dt_ref, cw_ref, cb_ref, dtb_ref, alog_ref,
                     dexp_ref, nw_ref, convp_ref, ssm0_ref,
                     yz_ref, ssmo_ref, act, xt, zt, yt, xd, ear, bgs, cgs):
    q = SSD_CHUNK
    tl = SSD_TOK
    hp = SSD_HEAD_DIM
    tile = pl.program_id(0)
    sub = pl.program_id(1)

    @pl.when(sub == 0)
    def _shared():
        rowt = lax.broadcasted_iota(jnp.int32, (q, CONV_SEC), 0) & (tl - 1)
        for s0 in range(0, SSD_CONV_DIM, CONV_SEC):
            cs = slice(s0, s0 + CONV_SEC)
            if s0 < SSD_WIDTH:
                x = xs_ref[:, cs]
            elif s0 < SSD_WIDTH + 1024:
                x = b_ref[:, s0 - SSD_WIDTH:s0 - SSD_WIDTH + CONV_SEC]
            else:
                x = c_ref[:, s0 - SSD_WIDTH - 1024:s0 - SSD_WIDTH - 1024 + CONV_SEC]
            e = convp_ref[:, :, cs].reshape(q, CONV_SEC)
            acc = cb_ref[:, cs] + cw_ref[SSD_CONV - 1:SSD_CONV, cs] * x
            for d in range(1, SSD_CONV):
                prev = jnp.where(rowt >= d, pltpu.roll(x, d, axis=0), pltpu.roll(e, q - tl + d, axis=0))
                acc = acc + cw_ref[SSD_CONV - 1 - d:SSD_CONV - d, cs] * prev
            act[:, cs] = _silu(acc)

        dt = jax.nn.softplus(dt_ref[...] + dtb_ref[...])
        da = dt * (-jnp.exp(alog_ref[...]))
        a_cs = _seg_cumsum_rows(da, tl)
        a_tot = a_cs + _seg_revsum_rows(da, tl) - da
        a_cst, a_tott, dtt = a_cs.T, a_tot.T, dt.T
        ea = jnp.exp(a_cst)
        _to_channel_major(act, xt)
        _to_channel_major(z_ref, zt)

        s_i = lax.broadcasted_iota(jnp.int32, (q, q), 0)
        l_i = lax.broadcasted_iota(jnp.int32, (q, q), 1)
        causal = (l_i >= s_i) & (jnp.right_shift(l_i, 3) == jnp.right_shift(s_i, 3))
        for g in range(SSD_GROUPS):
            bg, cg = _group_bc(act, g)
            bgs[g] = bg
            cgs[g] = cg
            cbt = _dot_nt(bg, cg)
            for r in range(SSD_HPG):
                h = g * SSD_HPG + r
                hs = slice(h * hp, (h + 1) * hp)
                arow = a_cst[h:h + 1, :]
                acol = a_cs[:, h:h + 1]
                lt = jnp.where(causal, jnp.exp(jnp.minimum(arow - acol, 0.0)), 0.0)
                wt = (cbt * lt).astype(BF16)
                xh = xt[hs, :]
                xdt = xh * dtt[h:h + 1, :]
                yt[hs, :] = _dot(xdt.astype(BF16), wt) + dexp_ref[hs, :] * xh
                xd[hs, :] = xdt * jnp.exp(a_tott[h:h + 1, :] - arow)
                ear[hs, :] = jnp.broadcast_to(ea[h:h + 1, :], (hp, q))

    lane_seq = jnp.right_shift(lax.broadcasted_iota(jnp.int32, (SSD_GW, q), 1), 3)

    def seq_body(sl, carry):
        sq = sub * SSD_SUB + sl
        inseq = lane_seq == sq
        dbase = (tile * SSD_PACK + sq) * SSD_HEADS
        for g in range(SSD_GROUPS):
            rows = slice(g * SSD_GW, (g + 1) * SSD_GW)
            stg = ssm0_ref[sl, rows, :]
            yo = _dot_nt(stg.astype(BF16), cgs[g])
            yt[rows, :] = yt[rows, :] + jnp.where(inseq, yo * ear[rows, :], 0.0)
            upd = _dot(jnp.where(inseq, xd[rows, :], 0.0).astype(BF16), bgs[g])
            for r in range(SSD_HPG):
                h = g * SSD_HPG + r
                ssmo_ref[sl, h * hp:(h + 1) * hp, :] = (dec_ref[dbase + h] * stg[r * hp:(r + 1) * hp, :]
                                                        + upd[r * hp:(r + 1) * hp, :])
        return carry

    lax.fori_loop(0, SSD_SUB, seq_body, 0)

    @pl.when(sub == SSD_PACK // SSD_SUB - 1)
    def _():
        _gate_norm_store(yt, zt, nw_ref, yz_ref)


def _ssd_packed(proj, nseq, consts, conv0, ssm0):
    m = proj.shape[0]
    ntile = nseq // SSD_PACK
    nsub = SSD_PACK // SSD_SUB
    dec = _ssd_decay(proj, nseq, consts[2], consts[3])[:, :SSD_HEADS].reshape(-1)
    convp = jnp.pad(conv0, ((0, 0), (SSD_TOK - (SSD_CONV - 1), 0), (0, 0)))
    chan = pltpu.VMEM((SSD_WIDTH, LANES), F32)
    st_spec = pl.BlockSpec((SSD_SUB, SSD_WIDTH, SSD_STATE), lambda i, s: (i * nsub + s, 0, 0))
    return pl.pallas_call(
        _ssd_packed_body,
        grid=(ntile, nsub),
        in_specs=[pl.BlockSpec(memory_space=pltpu.SMEM)]
        + _ssd_proj_specs(SSD_CHUNK, lambda i, s: i) + _ssd_const_specs(2)
        + [pl.BlockSpec((SSD_PACK, SSD_TOK, SSD_CONV_DIM), lambda i, s: (i, 0, 0)), st_spec],
        out_specs=[pl.BlockSpec((SSD_CHUNK, SSD_WIDTH), lambda i, s: (i, 0)), st_spec],
        out_shape=[jax.ShapeDtypeStruct((m, SSD_WIDTH), F32),
                   jax.ShapeDtypeStruct((nseq, SSD_WIDTH, SSD_STATE), F32)],
        scratch_shapes=[pltpu.VMEM((SSD_CHUNK, SSD_CONV_DIM), F32), chan, chan, chan, chan, chan,
                        pltpu.VMEM((SSD_GROUPS, SSD_CHUNK, SSD_STATE), BF16),
                        pltpu.VMEM((SSD_GROUPS, SSD_CHUNK, SSD_STATE), BF16)],
        compiler_params=_cparams(("arbitrary", "arbitrary")),
        name="ssd_packed",
    )(dec, proj, proj, proj, proj, proj, *consts, convp, ssm0)


S5_GB = 8


def _s5_body(*refs, q, scan):
    if scan:
        u_ref, toep_ref, bin_ref, coff_ref, aqr_ref, aqi_ref, y_ref, so_ref, sr_s, si_s = refs
        s0_ref = None
    else:
        u_ref, toep_ref, bin_ref, coff_ref, aqr_ref, aqi_ref, s0_ref, y_ref, so_ref = refs
    k = S5_GROUP_SIZE
    p = S5_STATE
    gp = S5_GB * p

    def chunk_in(gl):
        return u_ref[:, gl * k:(gl + 1) * k, :].reshape(q * k, LANES)

    if scan:
        for gl in range(S5_GB):
            bo = _dot(bin_ref[gl], chunk_in(gl).astype(BF16))
            sr_s[gl * p:(gl + 1) * p, :] = bo[:p, :]
            si_s[gl * p:(gl + 1) * p, :] = bo[p:, :]
        lane = lax.broadcasted_iota(jnp.int32, (gp, LANES), 1)
        xr, xi = sr_s[...], si_s[...]
        pr = aqr_ref[...].reshape(gp, LANES)
        pi = aqi_ref[...].reshape(gp, LANES)
        s = 1
        while s < LANES:
            m = lane >= s
            tr = jnp.where(m, pltpu.roll(xr, s, axis=1), 0.0)
            ti = jnp.where(m, pltpu.roll(xi, s, axis=1), 0.0)
            xr, xi = xr + (pr * tr - pi * ti), xi + (pr * ti + pi * tr)
            if 2 * s < LANES:
                pr, pi = pr * pr - pi * pi, 2.0 * pr * pi
            s *= 2
        so_ref[:, :p, :] = xr[:, LANES - 1:LANES].reshape(S5_GB, p, 1)
        so_ref[:, p:, :] = xi[:, LANES - 1:LANES].reshape(S5_GB, p, 1)
        m1 = lane >= 1
        sr_s[...] = jnp.where(m1, pltpu.roll(xr, 1, axis=1), 0.0)
        si_s[...] = jnp.where(m1, pltpu.roll(xi, 1, axis=1), 0.0)

    for gl in range(S5_GB):
        z = chunk_in(gl)
        if scan:
            sin_r, sin_i = sr_s[gl * p:(gl + 1) * p, :], si_s[gl * p:(gl + 1) * p, :]
        else:
            bo = _dot_hi(bin_ref[gl], z)
            aqr, aqi = aqr_ref[gl], aqi_ref[gl]
            sin_r, sin_i = s0_ref[gl, :p, :], s0_ref[gl, p:, :]
            so_ref[gl, :p, :] = aqr * sin_r - aqi * sin_i + bo[:p, :]
            so_ref[gl, p:, :] = aqr * sin_i + aqi * sin_r + bo[p:, :]
        s_in = jnp.concatenate([sin_r, sin_i], axis=0).astype(BF16)
        y = _dot(toep_ref[gl], z.astype(BF16)) + _dot(coff_ref[gl], s_in)
        y_ref[:, gl * k:(gl + 1) * k, :] = y.reshape(q, k, LANES)


def _s5(ut, q, scan, prep, s0t=None):
    ctot = ut.shape[2]
    nlb = ctot // LANES
    ngb = S5_GROUPS // S5_GB
    qk = q * S5_GROUP_SIZE
    gspec = lambda r, c, cb=0: pl.BlockSpec((S5_GB, r, c), lambda gb, lb: (gb, 0, cb))
    in_specs = [pl.BlockSpec((q, S5_GB * S5_GROUP_SIZE, LANES), lambda gb, lb: (0, gb, lb)),
                gspec(qk, qk), gspec(2 * S5_STATE, qk, (S5_Q - q) * S5_GROUP_SIZE // qk),
                gspec(qk, 2 * S5_STATE), gspec(S5_STATE, LANES), gspec(S5_STATE, LANES)]
    args = [ut, prep["toep"], prep["binb"] if scan else prep["bin"], prep["coff"], *prep["aq"][q]]
    scratch = []
    if scan:
        so_shape = jax.ShapeDtypeStruct((nlb, S5_GROUPS, 2 * S5_STATE, 1), F32)
        so_spec = pl.BlockSpec((None, S5_GB, 2 * S5_STATE, 1), lambda gb, lb: (lb, gb, 0, 0))
        scratch = [pltpu.VMEM((S5_GB * S5_STATE, LANES), F32)] * 2
    else:
        in_specs.append(gspec(2 * S5_STATE, LANES))
        args.append(s0t)
        so_shape = jax.ShapeDtypeStruct((S5_GROUPS, 2 * S5_STATE, LANES), F32)
        so_spec = gspec(2 * S5_STATE, LANES)
    return pl.pallas_call(
        functools.partial(_s5_body, q=q, scan=scan),
        grid=(ngb, nlb),
        in_specs=in_specs,
        out_specs=[pl.BlockSpec((q, S5_GB * S5_GROUP_SIZE, LANES), lambda gb, lb: (0, gb, lb)), so_spec],
        out_shape=[jax.ShapeDtypeStruct(ut.shape, F32), so_shape],
        scratch_shapes=scratch,
        compiler_params=_cparams(("arbitrary", "arbitrary")),
        name="s5_scan" if scan else "s5_step",
    )(*args)


OUT_ROWS = 256


def _out_body(y5_ref, u5_ref, z5_ref, ga_ref, gb_ref, yz_ref, x_ref, d5_ref, wglu_ref, bglu_ref,
              wd5_ref, wda_ref, wout_ref, fnw_ref, o_ref):
    y5 = jax.nn.gelu(y5_ref[...] + d5_ref[...] * u5_ref[...])
    y5 = y5 * _sigmoid(_dot(y5.astype(BF16), wglu_ref[...]) + bglu_ref[...])
    y5 = y5 * _silu(z5_ref[...])
    y_b = _dot(y5.astype(BF16), wd5_ref[...])
    y_a = _dot(yz_ref[...].astype(BF16), wda_ref[...])
    mix = _sigmoid(ga_ref[...]) * y_a + _sigmoid(gb_ref[...]) * y_b
    out = x_ref[...] + _dot(mix.astype(BF16), wout_ref[...])
    ms = jnp.mean(out * out, axis=-1, keepdims=True)
    o_ref[...] = out * lax.rsqrt(ms + NORM_EPS) * fnw_ref[...]


def _out(y5, proj, yz, x2d, prm):
    m = x2d.shape[0]
    rowb = lambda w, cb: pl.BlockSpec((OUT_ROWS, w), lambda i: (i, cb))
    const = lambda r, c: pl.BlockSpec((r, c), lambda i: (0, 0))
    return pl.pallas_call(
        _out_body,
        grid=(m // OUT_ROWS,),
        in_specs=[rowb(1024, 0), rowb(1024, CB_U5), rowb(1024, CB_Z5), rowb(1024, CB_GA), rowb(1024, CB_GB),
                  rowb(SSD_WIDTH, 0), rowb(1024, 0),
                  const(1, 1024), const(1024, 1024), const(1, 1024), const(1024, 1024),
                  const(SSD_WIDTH, 1024), const(1024, 1024), const(1, 1024)],
        out_specs=rowb(1024, 0),
        out_shape=jax.ShapeDtypeStruct((m, D_MODEL), F32),
        compiler_params=_cparams(("arbitrary",)),
        name="out_stage",
    )(y5, proj, proj, proj, proj, yz, x2d, prm["d_s5"], prm["w_glu"], prm["b_glu"],
      prm["w_down_s5"], prm["w_down_ssd"], prm["w_out"], prm["final_norm_w"])


def _run(x, conv0, ssm0, re0, im0, prm):
    nseq, seqlen, _ = x.shape
    m = nseq * seqlen
    prompt = conv0 is None
    x2d = x.reshape(m, D_MODEL)
    proj = _in_proj(x2d, prm["norm_w"], prm["w_cat"])
    conv_new = proj[:, SSD_WIDTH:SSD_WIDTH + SSD_CONV_DIM].reshape(nseq, seqlen, SSD_CONV_DIM)[:, 1 - SSD_CONV:]

    if prompt:
        yz, ssm_new = _ssd(proj, nseq, seqlen // SSD_CHUNK, prm["ssd_consts"])
    else:
        yz, ssm_new = _ssd_packed(proj, nseq, prm["ssd_consts"], conv0,
                                  ssm0.reshape(nseq, SSD_WIDTH, SSD_STATE))

    q = S5_Q if prompt else seqlen
    u5 = proj[:, CB_U5 * 1024:(CB_U5 + 1) * 1024]
    ut = jnp.transpose(u5.reshape(m // q, q, S5_WIDTH), (1, 2, 0))
    if prompt:
        yt, so = _s5(ut, q, True, prm["s5"])
        so = so.reshape(nseq, S5_GROUPS, 2, S5_STATE)
        re_new, im_new = so[:, :, 0, :], so[:, :, 1, :]
    else:
        s0t = jnp.transpose(jnp.concatenate([re0, im0], axis=-1), (1, 2, 0))
        yt, so = _s5(ut, q, False, prm["s5"], s0t)
        so = jnp.transpose(so, (2, 0, 1))
        re_new, im_new = so[:, :, :S5_STATE], so[:, :, S5_STATE:]
    y5 = jnp.transpose(yt, (2, 0, 1)).reshape(m, S5_WIDTH)

    out = _out(y5, proj, yz, x2d, prm)
    return (out.reshape(nseq, seqlen, D_MODEL), conv_new[None],
            ssm_new.reshape(1, nseq, SSD_HEADS, SSD_HEAD_DIM, SSD_STATE), re_new[None], im_new[None])


def kernel(x_prompt, x_sample, state_conv, state_ssm, state_s5_re, state_s5_im, norm_w, w_in, conv_w, conv_b,
           dt_bias, A_log, D_ssd, ssd_norm_w, w_down_ssd, lam_re, lam_im, log_dt, B_re, B_im, C_re, C_im, D_s5,
           w_glu, b_glu, w_down_s5, w_out, final_norm_w):
    assert norm_w.shape[0] == 1, "single-layer trunk"
    assert x_prompt.shape[1] % SSD_CHUNK == 0 and x_prompt.shape[1] // S5_Q == LANES
    assert x_sample.shape[1] == SSD_TOK and x_sample.shape[0] == LANES
    w = w_in[0]
    o_dt = SSD_WIDTH + SSD_CONV_DIM
    w_cat = jnp.concatenate([w[:, :o_dt], w[:, o_dt + SSD_HEADS:], w[:, o_dt:o_dt + SSD_HEADS],
                             jnp.zeros((D_MODEL, LANES - SSD_HEADS), F32)], axis=1).astype(BF16)
    prm = dict(norm_w=norm_w[0].reshape(1, -1), w_cat=w_cat,
               ssd_consts=_ssd_consts(conv_w[0], conv_b[0], dt_bias[0], A_log[0], D_ssd[0], ssd_norm_w[0]),
               s5=_s5_prep(lam_re[0], lam_im[0], log_dt[0], B_re[0], B_im[0], C_re[0], C_im[0]),
               d_s5=D_s5[0].reshape(1, -1), w_glu=w_glu[0].astype(BF16), b_glu=b_glu[0].reshape(1, -1),
               w_down_s5=w_down_s5[0].astype(BF16), w_down_ssd=w_down_ssd[0].astype(BF16),
               w_out=w_out[0].astype(BF16), final_norm_w=final_norm_w.reshape(1, -1))

    y_p, conv_p, ssm_p, re_p, im_p = _run(x_prompt, None, None, None, None, prm)
    y_s, conv_s, ssm_s, re_s, im_s = _run(x_sample, state_conv[0], state_ssm[0], state_s5_re[0],
                                          state_s5_im[0], prm)
    return (y_p, y_s, conv_p, ssm_p, re_p, im_p, conv_s, ssm_s, re_s, im_s)
```

```python
import functools

import jax
import jax.numpy as jnp
from jax import lax
from jax.experimental import pallas as pl
from jax.experimental.pallas import tpu as pltpu

F32 = jnp.float32
BF16 = jnp.bfloat16

D_MODEL = 1024
SSD_WIDTH = 2048
SSD_HEAD_DIM = 64
SSD_HEADS = 32
SSD_GROUPS = 8
SSD_HPG = SSD_HEADS // SSD_GROUPS
SSD_GW = SSD_WIDTH // SSD_GROUPS
SSD_STATE = 128
SSD_CONV = 4
SSD_CHUNK = 128
SSD_CONV_DIM = 4096
S5_WIDTH = 1024
S5_GROUP_SIZE = 16
S5_GROUPS = 64
S5_STATE = 64
NORM_EPS = 1e-6

LANES = 128
SUBLANES = 8

PROJ_COLS = 10240
COL_TILE = 1024
ROW_TILE = 1024
CB_B, CB_C, CB_U5, CB_Z5, CB_GA, CB_GB = 4, 5, 6, 7, 8, 9

VMEM_LIMIT = 56 * 1024 * 1024


def _cparams(sem):
    return pltpu.CompilerParams(dimension_semantics=sem, vmem_limit_bytes=VMEM_LIMIT)


def _sigmoid(x):
    return 0.5 * jnp.tanh(0.5 * x) + 0.5


def _silu(x):
    return x * _sigmoid(x)


def _dot(a, b):
    return jnp.dot(a, b, preferred_element_type=F32)


def _dot_nt(a, b):
    return lax.dot_general(a, b, (((1,), (1,)), ((), ())), preferred_element_type=F32)


def _dot_hi(a, b):
    a_hi = a.astype(BF16)
    a_lo = (a - a_hi.astype(F32)).astype(BF16)
    if b.dtype == BF16:
        return _dot(a_hi, b) + _dot(a_lo, b)
    b_hi = b.astype(BF16)
    b_lo = (b - b_hi.astype(F32)).astype(BF16)
    return _dot(a_hi, b_hi) + _dot(a_hi, b_lo) + _dot(a_lo, b_hi)


S5_Q = 16
PREP_GB = 8


def _zoh(lr, li, ldt):
    step = jnp.exp(ldt)
    mag = jnp.exp(lr * step)
    return mag * jnp.cos(li * step), mag * jnp.sin(li * step)


def _s5_prep_body(rowp_ref, lrc_ref, lic_ref, ldtc_ref, cre_ref, cim_ref, b1_ref, b2_ref,
                  toep_ref, coff_ref, bin_ref, binb_ref, aq16r_ref, aq16i_ref, aq8r_ref, aq8i_ref, ca_s):
    k, p, q = S5_GROUP_SIZE, S5_STATE, S5_Q
    qk = q * k
    lr, li = lrc_ref[...], lic_ref[...]
    ar, ai = _zoh(lr, li, ldtc_ref[...])
    den = lr * lr + li * li
    nr = ar - 1.0
    cr = (nr * lr + ai * li) / den
    ci = (ai * lr - nr * li) / den
    sgn_row = jnp.where(lax.broadcasted_iota(jnp.int32, (PREP_GB, 2 * p, 1), 1) < p, -1.0, 1.0)
    b1, b2 = b1_ref[...], b2_ref[...]
    bb = cr * b1 + sgn_row * ci * b2
    bb_sw = cr * b2 - sgn_row * ci * b1
    lane_t = jnp.right_shift(lax.broadcasted_iota(jnp.int32, (PREP_GB, 2 * p, qk), 2), 4)
    pr, pi = jnp.ones_like(ar), jnp.zeros_like(ar)
    pwr = jnp.zeros((PREP_GB, 2 * p, qk), F32)
    pwi = jnp.zeros((PREP_GB, 2 * p, qk), F32)
    for j in range(q + 1):
        if j < q:
            sel = lane_t == (q - 1 - j)
            pwr = jnp.where(sel, pr, pwr)
            pwi = jnp.where(sel, pi, pwi)
        if j == q // 2:
            aq8r_ref[...] = jnp.broadcast_to(pr[:, :p, :], aq8r_ref.shape)
            aq8i_ref[...] = jnp.broadcast_to(pi[:, :p, :], aq8i_ref.shape)
        if j == q:
            aq16r_ref[...] = jnp.broadcast_to(pr[:, :p, :], aq16r_ref.shape)
            aq16i_ref[...] = jnp.broadcast_to(pi[:, :p, :], aq16i_ref.shape)
        pr, pi = pr * ar - pi * ai, pr * ai + pi * ar
    binv = pwr * bb + sgn_row * pwi * bb_sw
    bin_ref[...] = binv
    binb_ref[...] = binv.astype(BF16)

    ar, ai = _zoh(rowp_ref[:, 0:1, :], rowp_ref[:, 1:2, :], rowp_ref[:, 2:3, :])
    low = lax.broadcasted_iota(jnp.int32, (PREP_GB, 1, 2 * p), 2) < p
    sgn_lane = jnp.where(low, 1.0, -1.0)
    cre, cim = cre_ref[...], cim_ref[...]
    pr, pi = jnp.ones_like(ar), jnp.zeros_like(ar)
    for j in range(q + 1):
        p1 = jnp.where(low, pr, pi)
        p2 = jnp.where(low, pi, pr)
        ca_s[:, j * k:(j + 1) * k, :] = sgn_lane * cre * p1 - cim * p2
        pr, pi = pr * ar - pi * ai, pr * ai + pi * ar
    coff_ref[...] = ca_s[:, k:, :].astype(BF16)

    lane_blk = jnp.right_shift(lax.broadcasted_iota(jnp.int32, (qk, qk), 1), 4)
    for g in range(PREP_GB):
        km = _dot_hi(ca_s[g, :qk, :], bb[g])
        tp = jnp.where(lane_blk == 0, km, 0.0)
        for t in range(1, q):
            sh = jnp.concatenate([jnp.zeros((t * k, qk), F32), km[:qk - t * k, :]], axis=0)
            tp = jnp.where(lane_blk == t, sh, tp)
        toep_ref[g] = tp.astype(BF16)


def _s5_prep(lam_re, lam_im, log_dt, b_re, b_im, c_re, c_im):
    g, p, k, q = S5_GROUPS, S5_STATE, S5_GROUP_SIZE, S5_Q
    qk = q * k
    dup = lambda a: jnp.concatenate([a, a], axis=-1)
    ldt = jnp.broadcast_to(log_dt[:, None], (g, 2 * p))
    rowp = jnp.stack([dup(lam_re), dup(lam_im), ldt], axis=1)
    b1 = jnp.tile(jnp.concatenate([b_re, b_im], axis=1), (1, 1, q))
    b2 = jnp.tile(jnp.concatenate([b_im, b_re], axis=1), (1, 1, q))
    blk = lambda r, c: pl.BlockSpec((PREP_GB, r, c), lambda i: (i, 0, 0))
    shp = lambda r, c, dt: jax.ShapeDtypeStruct((g, r, c), dt)
    toep, coff, bin_, binb, aq16r, aq16i, aq8r, aq8i = pl.pallas_call(
        _s5_prep_body,
        grid=(g // PREP_GB,),
        in_specs=[blk(3, 2 * p), blk(2 * p, 1), blk(2 * p, 1), blk(2 * p, 1),
                  blk(k, 2 * p), blk(k, 2 * p), blk(2 * p, qk), blk(2 * p, qk)],
        out_specs=[blk(qk, qk), blk(qk, 2 * p), blk(2 * p, qk), blk(2 * p, qk),
                   blk(p, LANES), blk(p, LANES), blk(p, LANES), blk(p, LANES)],
        out_shape=[shp(qk, qk, BF16), shp(qk, 2 * p, BF16), shp(2 * p, qk, F32), shp(2 * p, qk, BF16),
                   shp(p, LANES, F32), shp(p, LANES, F32), shp(p, LANES, F32), shp(p, LANES, F32)],
        scratch_shapes=[pltpu.VMEM((PREP_GB, (q + 1) * k, 2 * p), F32)],
        compiler_params=_cparams(("arbitrary",)),
        name="s5_prep",
    )(rowp, dup(lam_re)[:, :, None], dup(lam_im)[:, :, None], ldt[:, :, None],
      dup(c_re), dup(c_im), b1, b2)
    return dict(toep=toep, coff=coff, bin=bin_, binb=binb,
                aq={16: (aq16r, aq16i), 8: (aq8r, aq8i)})


def _in_proj_body(x_ref, nw_ref, w_ref, wdt_ref, o_ref, dt_ref, h_ref):
    @pl.when(pl.program_id(1) == 0)
    def _():
        x = x_ref[...]
        ms = jnp.mean(x * x, axis=-1, keepdims=True)
        h = (x * lax.rsqrt(ms + NORM_EPS) * nw_ref[...]).astype(BF16)
        h_ref[...] = h
        dt_ref[...] = _dot(h, wdt_ref[...])

    o_ref[...] = _dot(h_ref[...], w_ref[...]).astype(BF16)


def _in_proj(x2d, norm_w, w_cat, w_dt):
    m = x2d.shape[0]
    return pl.pallas_call(
        _in_proj_body,
        grid=(m // ROW_TILE, PROJ_COLS // COL_TILE),
        in_specs=[pl.BlockSpec((ROW_TILE, D_MODEL), lambda i, j: (i, 0)),
                  pl.BlockSpec((1, D_MODEL), lambda i, j: (0, 0)),
                  pl.BlockSpec((D_MODEL, COL_TILE), lambda i, j: (0, j)),
                  pl.BlockSpec((D_MODEL, LANES), lambda i, j: (0, 0))],
        out_specs=[pl.BlockSpec((ROW_TILE, COL_TILE), lambda i, j: (i, j)),
                   pl.BlockSpec((ROW_TILE, LANES), lambda i, j: (i, 0))],
        out_shape=[jax.ShapeDtypeStruct((m, PROJ_COLS), BF16), jax.ShapeDtypeStruct((m, LANES), F32)],
        scratch_shapes=[pltpu.VMEM((ROW_TILE, D_MODEL), BF16)],
        compiler_params=_cparams(("arbitrary", "arbitrary")),
        name="in_proj",
    )(x2d, norm_w, w_cat, w_dt)


CONV_KEEP = 2 * SUBLANES
CONV_SEC = 256


def _seg_cumsum_rows(x, seg):
    t = lax.broadcasted_iota(jnp.int32, x.shape, 0) & (seg - 1)
    s = 1
    while s < seg:
        x = x + jnp.where(t >= s, pltpu.roll(x, s, axis=0), 0.0)
        s *= 2
    return x


def _seg_revsum_rows(x, seg):
    n = x.shape[0]
    t = lax.broadcasted_iota(jnp.int32, x.shape, 0) & (seg - 1)
    s = 1
    while s < seg:
        x = x + jnp.where(t + s < seg, pltpu.roll(x, n - s, axis=0), 0.0)
        s *= 2
    return x


def _to_channel_major(src, dst):
    for j in range(SSD_WIDTH // LANES):
        sl = slice(j * LANES, (j + 1) * LANES)
        dst[sl, :] = src[:, sl].astype(F32).T


def _gate_norm_store(yt, zt, nw_ref, yz_ref):
    for g in range(SSD_GROUPS):
        sl = slice(g * SSD_GW, (g + 1) * SSD_GW)
        v = yt[sl, :] * _silu(zt[sl, :])
        ms = jnp.mean(v * v, axis=0, keepdims=True)
        yt[sl, :] = v * lax.rsqrt(ms + NORM_EPS) * nw_ref[sl, :]
    for j in range(SSD_WIDTH // LANES):
        sl = slice(j * LANES, (j + 1) * LANES)
        yz_ref[:, sl] = yt[sl, :].T.astype(yz_ref.dtype)


def _group_bc(act, g):
    bg = act[:, SSD_WIDTH + g * SSD_STATE:SSD_WIDTH + (g + 1) * SSD_STATE].astype(BF16)
    cg = act[:, SSD_WIDTH + 1024 + g * SSD_STATE:SSD_WIDTH + 1024 + (g + 1) * SSD_STATE].astype(BF16)
    return bg, cg


def _ssd_body(z_ref, xs_ref, b_ref, c_ref, dt_ref, cw_ref, cb_ref, dtb_ref, alog_ref, dexp_ref, nw_ref,
              yz_ref, ssmo_ref, cbuf, act, xt, zt, yt, state):
    q = SSD_CHUNK
    hp = SSD_HEAD_DIM

    @pl.when(pl.program_id(1) == 0)
    def _():
        state[...] = jnp.zeros_like(state)
        cbuf[q:, :] = jnp.zeros((q, SSD_CONV_DIM), BF16)

    cbuf[:q, 0:SSD_WIDTH] = xs_ref[...]
    cbuf[:q, SSD_WIDTH:SSD_WIDTH + 1024] = b_ref[...]
    cbuf[:q, SSD_WIDTH + 1024:] = c_ref[...]
    t_i = lax.broadcasted_iota(jnp.int32, (q, 2 * q), 0)
    r_i = lax.broadcasted_iota(jnp.int32, (q, 2 * q), 1)
    shifts = [jnp.where(r_i == ((t_i - d) & (2 * q - 1)), 1.0, 0.0).astype(BF16) for d in range(1, SSD_CONV)]
    for s0 in range(0, SSD_CONV_DIM, CONV_SEC):
        cs = slice(s0, s0 + CONV_SEC)
        acc = cb_ref[:, cs] + cw_ref[SSD_CONV - 1:SSD_CONV, cs] * cbuf[:q, cs].astype(F32)
        both = cbuf[:, cs]
        for d in range(1, SSD_CONV):
            acc = acc + cw_ref[SSD_CONV - 1 - d:SSD_CONV - d, cs] * _dot(shifts[d - 1], both)
        act[:, cs] = _silu(acc)
    cbuf[2 * q - CONV_KEEP:, :] = cbuf[q - CONV_KEEP:q, :]

    dt = jax.nn.softplus(dt_ref[...] + dtb_ref[...])
    a_cs = _seg_cumsum_rows(dt * (-jnp.exp(alog_ref[...])), q)
    a_cst = a_cs.T
    dtt = dt.T
    _to_channel_major(act, xt)
    _to_channel_major(z_ref, zt)

    causal = lax.broadcasted_iota(jnp.int32, (q, q), 1) >= lax.broadcasted_iota(jnp.int32, (q, q), 0)
    for g in range(SSD_GROUPS):
        bg, cg = _group_bc(act, g)
        cbt = _dot_nt(bg, cg)
        stg = state[g * SSD_GW:(g + 1) * SSD_GW, :]
        yoff = _dot_nt(stg.astype(BF16), cg)
        for r in range(SSD_HPG):
            h = g * SSD_HPG + r
            hs = slice(h * hp, (h + 1) * hp)
            arow = a_cst[h:h + 1, :]
            acol = a_cs[:, h:h + 1]
            lt = jnp.where(causal, jnp.exp(arow - acol), 0.0)
            wt = (cbt * lt).astype(BF16)
            xh = xt[hs, :]
            xdt = xh * dtt[h:h + 1, :]
            yt[hs, :] = (_dot(xdt.astype(BF16), wt) + yoff[r * hp:(r + 1) * hp, :] * jnp.exp(arow)
                         + dexp_ref[hs, :] * xh)
            atot = arow[:, q - 1:q]
            xdec = xdt * jnp.exp(atot - arow)
            state[hs, :] = jnp.exp(atot) * stg[r * hp:(r + 1) * hp, :] + _dot(xdec.astype(BF16), bg)

    ssmo_ref[...] = state[...]
    _gate_norm_store(yt, zt, nw_ref, yz_ref)


def _ssd_consts(conv_w, conv_b, dt_bias, a_log, d_ssd, ssd_norm_w):
    hrep = jnp.broadcast_to(jnp.repeat(d_ssd, SSD_HEAD_DIM)[:, None], (SSD_WIDTH, LANES))
    padl = lambda v: jnp.pad(v, (0, LANES - SSD_HEADS)).reshape(1, LANES)
    return (conv_w, conv_b.reshape(1, -1), padl(dt_bias), padl(a_log), hrep,
            jnp.broadcast_to(ssd_norm_w[:, None], (SSD_WIDTH, LANES)))


def _ssd_const_specs(n_grid):
    zeros = (0,) * 2
    const = lambda shape: pl.BlockSpec(shape, lambda *_: zeros)
    return [const((SSD_CONV, SSD_CONV_DIM)), const((1, SSD_CONV_DIM)), const((1, LANES)), const((1, LANES)),
            const((SSD_WIDTH, LANES)), const((SSD_WIDTH, LANES))]


def _ssd_proj_specs(rows, rowf):
    return [pl.BlockSpec((rows, SSD_WIDTH), lambda *a: (rowf(*a), 0)),
            pl.BlockSpec((rows, SSD_WIDTH), lambda *a: (rowf(*a), 1)),
            pl.BlockSpec((rows, 1024), lambda *a: (rowf(*a), CB_B)),
            pl.BlockSpec((rows, 1024), lambda *a: (rowf(*a), CB_C)),
            pl.BlockSpec((rows, LANES), lambda *a: (rowf(*a), 0))]


def _ssd(proj, dt, nseq, nchunk, consts):
    m = proj.shape[0]
    rowf = lambda b, c: b * nchunk + c
    chan = pltpu.VMEM((SSD_WIDTH, LANES), F32)
    return pl.pallas_call(
        _ssd_body,
        grid=(nseq, nchunk),
        in_specs=_ssd_proj_specs(SSD_CHUNK, rowf) + _ssd_const_specs(2),
        out_specs=[pl.BlockSpec((SSD_CHUNK, SSD_WIDTH), lambda b, c: (rowf(b, c), 0)),
                   pl.BlockSpec((None, SSD_WIDTH, SSD_STATE), lambda b, c: (b, 0, 0))],
        out_shape=[jax.ShapeDtypeStruct((m, SSD_WIDTH), BF16),
                   jax.ShapeDtypeStruct((nseq, SSD_WIDTH, SSD_STATE), F32)],
        scratch_shapes=[pltpu.VMEM((2 * SSD_CHUNK, SSD_CONV_DIM), BF16),
                        pltpu.VMEM((SSD_CHUNK, SSD_CONV_DIM), F32),
                        chan, chan, chan, pltpu.VMEM((SSD_WIDTH, SSD_STATE), F32)],
        compiler_params=_cparams(("arbitrary", "arbitrary")),
        name="ssd",
    )(proj, proj, proj, proj, dt, *consts)


SSD_PACK = 16
SSD_TOK = SSD_CHUNK // SSD_PACK
SSD_SUB = 4


def _ssd_decay_body(dt_ref, dtb_ref, alog_ref, o_ref):
    da = jax.nn.softplus(dt_ref[...] + dtb_ref[...]) * (-jnp.exp(alog_ref[...]))
    o_ref[...] = jnp.exp(jnp.sum(da.reshape(o_ref.shape[0], SSD_TOK, LANES), axis=1))


def _ssd_decay(dt, nseq, dtb, alog):
    const = lambda shape: pl.BlockSpec(shape, lambda i: (0, 0))
    return pl.pallas_call(
        _ssd_decay_body,
        grid=(1,),
        in_specs=[const((nseq * SSD_TOK, LANES)), const((1, LANES)), const((1, LANES))],
        out_specs=const((nseq, LANES)),
        out_shape=jax.ShapeDtypeStruct((nseq, LANES), F32),
        compiler_params=_cparams(("arbitrary",)),
        name="ssd_decay",
    )(dt, dtb, alog)


def _ssd_packed_body(dec_ref, z_ref, xs_ref, b_ref, c_ref, dt_ref, cw_ref, cb_ref, dtb_ref, alog_ref,
                     dexp_ref, nw_ref, convp_ref, ssm0_ref,
                     yz_ref, ssmo_ref, act, xt, zt, yt, xd, ear, bgs, cgs):
    q = SSD_CHUNK
    tl = SSD_TOK
    hp = SSD_HEAD_DIM
    tile = pl.program_id(0)
    sub = pl.program_id(1)

    @pl.when(sub == 0)
    def _shared():
        rowt = lax.broadcasted_iota(jnp.int32, (q, CONV_SEC), 0) & (tl - 1)
        for s0 in range(0, SSD_CONV_DIM, CONV_SEC):
            cs = slice(s0, s0 + CONV_SEC)
            if s0 < SSD_WIDTH:
                x = xs_ref[:, cs]
            elif s0 < SSD_WIDTH + 1024:
                x = b_ref[:, s0 - SSD_WIDTH:s0 - SSD_WIDTH + CONV_SEC]
            else:
                x = c_ref[:, s0 - SSD_WIDTH - 1024:s0 - SSD_WIDTH - 1024 + CONV_SEC]
            x = x.astype(F32)
            e = convp_ref[:, :, cs].reshape(q, CONV_SEC)
            acc = cb_ref[:, cs] + cw_ref[SSD_CONV - 1:SSD_CONV, cs] * x
            for d in range(1, SSD_CONV):
                prev = jnp.where(rowt >= d, pltpu.roll(x, d, axis=0), pltpu.roll(e, q - tl + d, axis=0))
                acc = acc + cw_ref[SSD_CONV - 1 - d:SSD_CONV - d, cs] * prev
            act[:, cs] = _silu(acc)

        dt = jax.nn.softplus(dt_ref[...] + dtb_ref[...])
        da = dt * (-jnp.exp(alog_ref[...]))
        a_cs = _seg_cumsum_rows(da, tl)
        a_tot = a_cs + _seg_revsum_rows(da, tl) - da
        a_cst, a_tott, dtt = a_cs.T, a_tot.T, dt.T
        ea = jnp.exp(a_cst)
        _to_channel_major(act, xt)
        _to_channel_major(z_ref, zt)

        s_i = lax.broadcasted_iota(jnp.int32, (q, q), 0)
        l_i = lax.broadcasted_iota(jnp.int32, (q, q), 1)
        causal = (l_i >= s_i) & (jnp.right_shift(l_i, 3) == jnp.right_shift(s_i, 3))
        for g in range(SSD_GROUPS):
            bg, cg = _group_bc(act, g)
            bgs[g] = bg
            cgs[g] = cg
            cbt = _dot_nt(bg, cg)
            for r in range(SSD_HPG):
                h = g * SSD_HPG + r
                hs = slice(h * hp, (h + 1) * hp)
                arow = a_cst[h:h + 1, :]
                acol = a_cs[:, h:h + 1]
                lt = jnp.where(causal, jnp.exp(arow - acol), 0.0)
                wt = (cbt * lt).astype(BF16)
                xh = xt[hs, :]
                xdt = xh * dtt[h:h + 1, :]
                yt[hs, :] = _dot(xdt.astype(BF16), wt) + dexp_ref[hs, :] * xh
                xd[hs, :] = xdt * jnp.exp(a_tott[h:h + 1, :] - arow)
                ear[hs, :] = jnp.broadcast_to(ea[h:h + 1, :], (hp, q))

    lane_seq = jnp.right_shift(lax.broadcasted_iota(jnp.int32, (SSD_GW, q), 1), 3)

    def seq_body(sl, carry):
        sq = sub * SSD_SUB + sl
        inseq = lane_seq == sq
        dbase = (tile * SSD_PACK + sq) * SSD_HEADS
        for g in range(SSD_GROUPS):
            rows = slice(g * SSD_GW, (g + 1) * SSD_GW)
            stg = ssm0_ref[sl, rows, :]
            yo = _dot_nt(stg.astype(BF16), cgs[g])
            yt[rows, :] = yt[rows, :] + jnp.where(inseq, yo * ear[rows, :], 0.0)
            upd = _dot(jnp.where(inseq, xd[rows, :], 0.0).astype(BF16), bgs[g])
            for r in range(SSD_HPG):
                h = g * SSD_HPG + r
                ssmo_ref[sl, h * hp:(h + 1) * hp, :] = (dec_ref[dbase + h] * stg[r * hp:(r + 1) * hp, :]
                                                        + upd[r * hp:(r + 1) * hp, :])
        return carry

    lax.fori_loop(0, SSD_SUB, seq_body, 0)

    @pl.when(sub == SSD_PACK // SSD_SUB - 1)
    def _():
        _gate_norm_store(yt, zt, nw_ref, yz_ref)


def _ssd_packed(proj, dt, nseq, consts, conv0, ssm0):
    m = proj.shape[0]
    ntile = nseq // SSD_PACK
    nsub = SSD_PACK // SSD_SUB
    dec = _ssd_decay(dt, nseq, consts[2], consts[3])[:, :SSD_HEADS].reshape(-1)
    convp = jnp.pad(conv0, ((0, 0), (SSD_TOK - (SSD_CONV - 1), 0), (0, 0)))
    chan = pltpu.VMEM((SSD_WIDTH, LANES), F32)
    st_spec = pl.BlockSpec((SSD_SUB, SSD_WIDTH, SSD_STATE), lambda i, s: (i * nsub + s, 0, 0))
    return pl.pallas_call(
        _ssd_packed_body,
        grid=(ntile, nsub),
        in_specs=[pl.BlockSpec(memory_space=pltpu.SMEM)]
        + _ssd_proj_specs(SSD_CHUNK, lambda i, s: i) + _ssd_const_specs(2)
        + [pl.BlockSpec((SSD_PACK, SSD_TOK, SSD_CONV_DIM), lambda i, s: (i, 0, 0)), st_spec],
        out_specs=[pl.BlockSpec((SSD_CHUNK, SSD_WIDTH), lambda i, s: (i, 0)), st_spec],
        out_shape=[jax.ShapeDtypeStruct((m, SSD_WIDTH), BF16),
                   jax.ShapeDtypeStruct((nseq, SSD_WIDTH, SSD_STATE), F32)],
        scratch_shapes=[pltpu.VMEM((SSD_CHUNK, SSD_CONV_DIM), F32), chan, chan, chan, chan, chan,
                        pltpu.VMEM((SSD_GROUPS, SSD_CHUNK, SSD_STATE), BF16),
                        pltpu.VMEM((SSD_GROUPS, SSD_CHUNK, SSD_STATE), BF16)],
        compiler_params=_cparams(("arbitrary", "arbitrary")),
        name="ssd_packed",
    )(dec, proj, proj, proj, proj, dt, *consts, convp, ssm0)


S5_GB = 8


def _s5_body(*refs, q, scan):
    if scan:
        u_ref, toep_ref, bin_ref, coff_ref, aqr_ref, aqi_ref, y_ref, so_ref, sr_s, si_s = refs
        s0_ref = None
    else:
        u_ref, toep_ref, bin_ref, coff_ref, aqr_ref, aqi_ref, s0_ref, y_ref, so_ref = refs
    k = S5_GROUP_SIZE
    p = S5_STATE
    gp = S5_GB * p

    def chunk_in(gl):
        return u_ref[:, gl * k:(gl + 1) * k, :].reshape(q * k, LANES)

    if scan:
        for gl in range(S5_GB):
            bo = _dot(bin_ref[gl], chunk_in(gl).astype(BF16))
            sr_s[gl * p:(gl + 1) * p, :] = bo[:p, :]
            si_s[gl * p:(gl + 1) * p, :] = bo[p:, :]
        lane = lax.broadcasted_iota(jnp.int32, (gp, LANES), 1)
        xr, xi = sr_s[...], si_s[...]
        pr = aqr_ref[...].reshape(gp, LANES)
        pi = aqi_ref[...].reshape(gp, LANES)
        s = 1
        while s < LANES:
            m = lane >= s
            tr = jnp.where(m, pltpu.roll(xr, s, axis=1), 0.0)
            ti = jnp.where(m, pltpu.roll(xi, s, axis=1), 0.0)
            xr, xi = xr + (pr * tr - pi * ti), xi + (pr * ti + pi * tr)
            if 2 * s < LANES:
                pr, pi = pr * pr - pi * pi, 2.0 * pr * pi
            s *= 2
        so_ref[:, :p, :] = xr[:, LANES - 1:LANES].reshape(S5_GB, p, 1)
        so_ref[:, p:, :] = xi[:, LANES - 1:LANES].reshape(S5_GB, p, 1)
        m1 = lane >= 1
        sr_s[...] = jnp.where(m1, pltpu.roll(xr, 1, axis=1), 0.0)
        si_s[...] = jnp.where(m1, pltpu.roll(xi, 1, axis=1), 0.0)

    for gl in range(S5_GB):
        z = chunk_in(gl)
        if scan:
            sin_r, sin_i = sr_s[gl * p:(gl + 1) * p, :], si_s[gl * p:(gl + 1) * p, :]
        else:
            bo = _dot_hi(bin_ref[gl], z)
            aqr, aqi = aqr_ref[gl], aqi_ref[gl]
            sin_r, sin_i = s0_ref[gl, :p, :], s0_ref[gl, p:, :]
            so_ref[gl, :p, :] = aqr * sin_r - aqi * sin_i + bo[:p, :]
            so_ref[gl, p:, :] = aqr * sin_i + aqi * sin_r + bo[p:, :]
        s_in = jnp.concatenate([sin_r, sin_i], axis=0).astype(BF16)
        y = _dot(toep_ref[gl], z.astype(BF16)) + _dot(coff_ref[gl], s_in)
        y_ref[:, gl * k:(gl + 1) * k, :] = y.reshape(q, k, LANES).astype(y_ref.dtype)


def _s5(ut, q, scan, prep, s0t=None):
    ctot = ut.shape[2]
    nlb = ctot // LANES
    ngb = S5_GROUPS // S5_GB
    qk = q * S5_GROUP_SIZE
    gspec = lambda r, c, cb=0: pl.BlockSpec((S5_GB, r, c), lambda gb, lb: (gb, 0, cb))
    in_specs = [pl.BlockSpec((q, S5_GB * S5_GROUP_SIZE, LANES), lambda gb, lb: (0, gb, lb)),
                gspec(qk, qk), gspec(2 * S5_STATE, qk, (S5_Q - q) * S5_GROUP_SIZE // qk),
                gspec(qk, 2 * S5_STATE), gspec(S5_STATE, LANES), gspec(S5_STATE, LANES)]
    args = [ut, prep["toep"], prep["binb"] if scan else prep["bin"], prep["coff"], *prep["aq"][q]]
    scratch = []
    if scan:
        so_shape = jax.ShapeDtypeStruct((nlb, S5_GROUPS, 2 * S5_STATE, 1), F32)
        so_spec = pl.BlockSpec((None, S5_GB, 2 * S5_STATE, 1), lambda gb, lb: (lb, gb, 0, 0))
        scratch = [pltpu.VMEM((S5_GB * S5_STATE, LANES), F32)] * 2
    else:
        in_specs.append(gspec(2 * S5_STATE, LANES))
        args.append(s0t)
        so_shape = jax.ShapeDtypeStruct((S5_GROUPS, 2 * S5_STATE, LANES), F32)
        so_spec = gspec(2 * S5_STATE, LANES)
    return pl.pallas_call(
        functools.partial(_s5_body, q=q, scan=scan),
        grid=(ngb, nlb),
        in_specs=in_specs,
        out_specs=[pl.BlockSpec((q, S5_GB * S5_GROUP_SIZE, LANES), lambda gb, lb: (0, gb, lb)), so_spec],
        out_shape=[jax.ShapeDtypeStruct(ut.shape, BF16), so_shape],
        scratch_shapes=scratch,
        compiler_params=_cparams(("arbitrary", "arbitrary")),
        name="s5_scan" if scan else "s5_step",
    )(*args)


OUT_ROWS = 512


def _out_body(y5_ref, u5_ref, z5_ref, ga_ref, gb_ref, yz_ref, x_ref, d5_ref, wglu_ref, bglu_ref,
              wd5_ref, wda_ref, wout_ref, fnw_ref, o_ref):
    f32 = lambda r: r[...].astype(F32)
    y5 = jax.nn.gelu(f32(y5_ref) + d5_ref[...] * f32(u5_ref))
    y5 = y5 * _sigmoid(_dot(y5.astype(BF16), wglu_ref[...]) + bglu_ref[...])
    y5 = y5 * _silu(f32(z5_ref))
    y_b = _dot(y5.astype(BF16), wd5_ref[...])
    y_a = _dot(yz_ref[...], wda_ref[...])
    mix = _sigmoid(f32(ga_ref)) * y_a + _sigmoid(f32(gb_ref)) * y_b
    out = x_ref[...] + _dot(mix.astype(BF16), wout_ref[...])
    ms = jnp.mean(out * out, axis=-1, keepdims=True)
    o_ref[...] = out * lax.rsqrt(ms + NORM_EPS) * fnw_ref[...]


def _out(y5, proj, yz, x2d, prm):
    m = x2d.shape[0]
    rowb = lambda w, cb: pl.BlockSpec((OUT_ROWS, w), lambda i: (i, cb))
    const = lambda r, c: pl.BlockSpec((r, c), lambda i: (0, 0))
    return pl.pallas_call(
        _out_body,
        grid=(m // OUT_ROWS,),
        in_specs=[rowb(1024, 0), rowb(1024, CB_U5), rowb(1024, CB_Z5), rowb(1024, CB_GA), rowb(1024, CB_GB),
                  rowb(SSD_WIDTH, 0), rowb(1024, 0),
                  const(1, 1024), const(1024, 1024), const(1, 1024), const(1024, 1024),
                  const(SSD_WIDTH, 1024), const(1024, 1024), const(1, 1024)],
        out_specs=rowb(1024, 0),
        out_shape=jax.ShapeDtypeStruct((m, D_MODEL), F32),
        compiler_params=_cparams(("arbitrary",)),
        name="out_stage",
    )(y5, proj, proj, proj, proj, yz, x2d, prm["d_s5"], prm["w_glu"], prm["b_glu"],
      prm["w_down_s5"], prm["w_down_ssd"], prm["w_out"], prm["final_norm_w"])


def _run(x, conv0, ssm0, re0, im0, prm):
    nseq, seqlen, _ = x.shape
    m = nseq * seqlen
    prompt = conv0 is None
    x2d = x.reshape(m, D_MODEL)
    proj, dt = _in_proj(x2d, prm["norm_w"], prm["w_cat"], prm["w_dt"])
    conv_new = proj.reshape(nseq, seqlen, PROJ_COLS)[:, seqlen - (SSD_CONV - 1):,
                                                     SSD_WIDTH:SSD_WIDTH + SSD_CONV_DIM].astype(F32)

    if prompt:
        yz, ssm_new = _ssd(proj, dt, nseq, seqlen // SSD_CHUNK, prm["ssd_consts"])
    else:
        yz, ssm_new = _ssd_packed(proj, dt, nseq, prm["ssd_consts"], conv0,
                                  ssm0.reshape(nseq, SSD_WIDTH, SSD_STATE))

    q = S5_Q if prompt else seqlen
    u5 = proj[:, CB_U5 * 1024:(CB_U5 + 1) * 1024]
    ut = jnp.transpose(u5.reshape(m // q, q, S5_WIDTH), (1, 2, 0))
    if prompt:
        yt, so = _s5(ut, q, True, prm["s5"])
        so = so.reshape(nseq, S5_GROUPS, 2, S5_STATE)
        re_new, im_new = so[:, :, 0, :], so[:, :, 1, :]
    else:
        s0t = jnp.transpose(jnp.concatenate([re0, im0], axis=-1), (1, 2, 0))
        yt, so = _s5(ut, q, False, prm["s5"], s0t)
        so = jnp.transpose(so, (2, 0, 1))
        re_new, im_new = so[:, :, :S5_STATE], so[:, :, S5_STATE:]
    y5 = jnp.transpose(yt, (2, 0, 1)).reshape(m, S5_WIDTH)

    out = _out(y5, proj, yz, x2d, prm)
    return (out.reshape(nseq, seqlen, D_MODEL), conv_new[None],
            ssm_new.reshape(1, nseq, SSD_HEADS, SSD_HEAD_DIM, SSD_STATE), re_new[None], im_new[None])


def kernel(x_prompt, x_sample, state_conv, state_ssm, state_s5_re, state_s5_im, norm_w, w_in, conv_w, conv_b,
           dt_bias, A_log, D_ssd, ssd_norm_w, w_down_ssd, lam_re, lam_im, log_dt, B_re, B_im, C_re, C_im, D_s5,
           w_glu, b_glu, w_down_s5, w_out, final_norm_w):
    assert norm_w.shape[0] == 1, "single-layer trunk"
    assert x_prompt.shape[1] % SSD_CHUNK == 0 and x_prompt.shape[1] // S5_Q == LANES
    assert x_sample.shape[1] == SSD_TOK and x_sample.shape[0] == LANES
    w = w_in[0]
    o_dt = SSD_WIDTH + SSD_CONV_DIM
    w_cat = jnp.concatenate([w[:, :o_dt], w[:, o_dt + SSD_HEADS:]], axis=1).astype(BF16)
    w_dt = jnp.pad(w[:, o_dt:o_dt + SSD_HEADS], ((0, 0), (0, LANES - SSD_HEADS))).astype(BF16)
    prm = dict(norm_w=norm_w[0].reshape(1, -1), w_cat=w_cat, w_dt=w_dt,
               ssd_consts=_ssd_consts(conv_w[0], conv_b[0], dt_bias[0], A_log[0], D_ssd[0], ssd_norm_w[0]),
               s5=_s5_prep(lam_re[0], lam_im[0], log_dt[0], B_re[0], B_im[0], C_re[0], C_im[0]),
               d_s5=D_s5[0].reshape(1, -1), w_glu=w_glu[0].astype(BF16), b_glu=b_glu[0].reshape(1, -1),
               w_down_s5=w_down_s5[0].astype(BF16), w_down_ssd=w_down_ssd[0].astype(BF16),
               w_out=w_out[0].astype(BF16), final_norm_w=final_norm_w.reshape(1, -1))

    y_p, conv_p, ssm_p, re_p, im_p = _run(x_prompt, None, None, None, None, prm)
    y_s, conv_s, ssm_s, re_s, im_s = _run(x_sample, state_conv[0], state_ssm[0], state_s5_re[0],
                                          state_s5_im[0], prm)
    return (y_p, y_s, conv_p, ssm_p, re_p, im_p, conv_s, ssm_s, re_s, im_s)
```

```python
import functools

import jax
import jax.numpy as jnp
from jax import lax
from jax.experimental import pallas as pl
from jax.experimental.pallas import tpu as pltpu

F32 = jnp.float32
BF16 = jnp.bfloat16

D_MODEL = 1024
SSD_WIDTH = 2048
SSD_HEAD_DIM = 64
SSD_HEADS = 32
SSD_GROUPS = 8
SSD_HPG = SSD_HEADS // SSD_GROUPS
SSD_GW = SSD_WIDTH // SSD_GROUPS
SSD_STATE = 128
SSD_CONV = 4
SSD_CHUNK = 128
SSD_CONV_DIM = 4096
S5_WIDTH = 1024
S5_GROUP_SIZE = 16
S5_GROUPS = 64
S5_STATE = 64
NORM_EPS = 1e-6
LOG2E = 1.4426950408889634

LANES = 128
SUBLANES = 8

PROJ_COLS = 10240
COL_TILE = 1024
ROW_TILE = 2048
CB_B, CB_C, CB_U5, CB_Z5, CB_GA, CB_GB = 4, 5, 6, 7, 8, 9

VMEM_LIMIT = 56 * 1024 * 1024


def _cparams(sem):
    return pltpu.CompilerParams(dimension_semantics=sem, vmem_limit_bytes=VMEM_LIMIT)


def _sigmoid(x):
    return 0.5 * jnp.tanh(0.5 * x) + 0.5


def _silu(x):
    return x * _sigmoid(x)


def _dot(a, b):
    return jnp.dot(a, b, preferred_element_type=F32)


def _dot_nt(a, b):
    return lax.dot_general(a, b, (((1,), (1,)), ((), ())), preferred_element_type=F32)


def _dot_hi(a, b):
    a_hi = a.astype(BF16)
    a_lo = (a - a_hi.astype(F32)).astype(BF16)
    if b.dtype == BF16:
        return _dot(a_hi, b) + _dot(a_lo, b)
    b_hi = b.astype(BF16)
    b_lo = (b - b_hi.astype(F32)).astype(BF16)
    return _dot(a_hi, b_hi) + _dot(a_hi, b_lo) + _dot(a_lo, b_hi)


S5_Q = 16
PREP_GB = 8


def _zoh(lr, li, ldt):
    step = jnp.exp(ldt)
    mag = jnp.exp(lr * step)
    return mag * jnp.cos(li * step), mag * jnp.sin(li * step)


def _s5_prep_body(rowp_ref, cre_ref, cim_ref, b1_ref, b2_ref,
                  toep_ref, coff_ref, bin_ref, binb_ref, aq16r_ref, aq16i_ref, aq8r_ref, aq8i_ref, ca_s):
    k, p, q = S5_GROUP_SIZE, S5_STATE, S5_Q
    qk = q * k
    lr, li = rowp_ref[:, 0:1, :], rowp_ref[:, 1:2, :]
    ar_l, ai_l = _zoh(lr, li, rowp_ref[:, 2:3, :])
    den = lr * lr + li * li
    nr = ar_l - 1.0
    cr_l = (nr * lr + ai_l * li) / den
    ci_l = (ai_l * lr - nr * li) / den

    def on_sublanes(v):
        return jnp.stack([jnp.broadcast_to(v[g], (2 * p, 2 * p)).T for g in range(PREP_GB)])

    wide = lambda v: jnp.concatenate([v] * (qk // (2 * p)), axis=-1)

    ar, ai, cr, ci = on_sublanes(ar_l), on_sublanes(ai_l), on_sublanes(cr_l), on_sublanes(ci_l)
    sgn_row = jnp.where(lax.broadcasted_iota(jnp.int32, (PREP_GB, 2 * p, qk), 1) < p, -1.0, 1.0)
    b1, b2 = b1_ref[...], b2_ref[...]
    crw, ciw = wide(cr), sgn_row * wide(ci)
    bb = crw * b1 + ciw * b2
    bb_sw = crw * b2 - ciw * b1
    lane_t = jnp.right_shift(lax.broadcasted_iota(jnp.int32, (PREP_GB, 2 * p, qk), 2), 4)
    pr, pi = jnp.ones_like(ar), jnp.zeros_like(ar)
    pwr = jnp.zeros((PREP_GB, 2 * p, qk), F32)
    pwi = jnp.zeros((PREP_GB, 2 * p, qk), F32)
    for j in range(q + 1):
        if j < q:
            sel = lane_t == (q - 1 - j)
            pwr = jnp.where(sel, wide(pr), pwr)
            pwi = jnp.where(sel, wide(pi), pwi)
        if j == q // 2:
            aq8r_ref[...] = pr[:, :p, :]
            aq8i_ref[...] = pi[:, :p, :]
        if j == q:
            aq16r_ref[...] = pr[:, :p, :]
            aq16i_ref[...] = pi[:, :p, :]
        pr, pi = pr * ar - pi * ai, pr * ai + pi * ar
    binv = pwr * bb + sgn_row * pwi * bb_sw
    bin_ref[...] = binv
    binb_ref[...] = binv.astype(BF16)

    ar, ai = ar_l, ai_l
    low = lax.broadcasted_iota(jnp.int32, (PREP_GB, 1, 2 * p), 2) < p
    sgn_lane = jnp.where(low, 1.0, -1.0)
    cre, cim = cre_ref[...], cim_ref[...]
    pr, pi = jnp.ones_like(ar), jnp.zeros_like(ar)
    for j in range(q + 1):
        p1 = jnp.where(low, pr, pi)
        p2 = jnp.where(low, pi, pr)
        ca_s[:, j * k:(j + 1) * k, :] = sgn_lane * cre * p1 - cim * p2
        pr, pi = pr * ar - pi * ai, pr * ai + pi * ar
    coff_ref[...] = ca_s[:, k:, :].astype(BF16)

    lane_blk = jnp.right_shift(lax.broadcasted_iota(jnp.int32, (qk, qk), 1), 4)
    for g in range(PREP_GB):
        km = _dot_hi(ca_s[g, :qk, :], bb[g])
        tp = jnp.where(lane_blk == 0, km, 0.0)
        for t in range(1, q):
            sh = jnp.concatenate([jnp.zeros((t * k, qk), F32), km[:qk - t * k, :]], axis=0)
            tp = jnp.where(lane_blk == t, sh, tp)
        toep_ref[g] = tp.astype(BF16)


def _s5_prep(lam_re, lam_im, log_dt, b_re, b_im, c_re, c_im):
    g, p, k, q = S5_GROUPS, S5_STATE, S5_GROUP_SIZE, S5_Q
    qk = q * k
    dup = lambda a: jnp.concatenate([a, a], axis=-1)
    ldt = jnp.broadcast_to(log_dt[:, None], (g, 2 * p))
    rowp = jnp.stack([dup(lam_re), dup(lam_im), ldt], axis=1)
    b1 = jnp.tile(jnp.concatenate([b_re, b_im], axis=1), (1, 1, q))
    b2 = jnp.tile(jnp.concatenate([b_im, b_re], axis=1), (1, 1, q))
    blk = lambda r, c: pl.BlockSpec((PREP_GB, r, c), lambda i: (i, 0, 0))
    shp = lambda r, c, dt: jax.ShapeDtypeStruct((g, r, c), dt)
    toep, coff, bin_, binb, aq16r, aq16i, aq8r, aq8i = pl.pallas_call(
        _s5_prep_body,
        grid=(g // PREP_GB,),
        in_specs=[blk(3, 2 * p), blk(k, 2 * p), blk(k, 2 * p), blk(2 * p, qk), blk(2 * p, qk)],
        out_specs=[blk(qk, qk), blk(qk, 2 * p), blk(2 * p, qk), blk(2 * p, qk),
                   blk(p, LANES), blk(p, LANES), blk(p, LANES), blk(p, LANES)],
        out_shape=[shp(qk, qk, BF16), shp(qk, 2 * p, BF16), shp(2 * p, qk, F32), shp(2 * p, qk, BF16),
                   shp(p, LANES, F32), shp(p, LANES, F32), shp(p, LANES, F32), shp(p, LANES, F32)],
        scratch_shapes=[pltpu.VMEM((PREP_GB, (q + 1) * k, 2 * p), F32)],
        compiler_params=_cparams(("arbitrary",)),
        name="s5_prep",
    )(rowp, dup(c_re), dup(c_im), b1, b2)
    return dict(toep=toep, coff=coff, bin=bin_, binb=binb,
                aq={16: (aq16r, aq16i), 8: (aq8r, aq8i)})


def _in_proj_body(x_ref, nw_ref, w_ref, wdt_ref, o_ref, dt_ref, h_ref):
    @pl.when(pl.program_id(1) == 0)
    def _():
        x = x_ref[...]
        ms = jnp.mean(x * x, axis=-1, keepdims=True)
        h = (x * lax.rsqrt(ms + NORM_EPS) * nw_ref[...]).astype(BF16)
        h_ref[...] = h
        dt_ref[...] = _dot(h, wdt_ref[...])

    o_ref[...] = _dot(h_ref[...], w_ref[...]).astype(BF16)


def _in_proj(x2d, norm_w, w_cat, w_dt):
    m = x2d.shape[0]
    rt = min(ROW_TILE, m)
    return pl.pallas_call(
        _in_proj_body,
        grid=(m // rt, PROJ_COLS // COL_TILE),
        in_specs=[pl.BlockSpec((rt, D_MODEL), lambda i, j: (i, 0)),
                  pl.BlockSpec((1, D_MODEL), lambda i, j: (0, 0)),
                  pl.BlockSpec((D_MODEL, COL_TILE), lambda i, j: (0, j)),
                  pl.BlockSpec((D_MODEL, LANES), lambda i, j: (0, 0))],
        out_specs=[pl.BlockSpec((rt, COL_TILE), lambda i, j: (i, j)),
                   pl.BlockSpec((rt, LANES), lambda i, j: (i, 0))],
        out_shape=[jax.ShapeDtypeStruct((m, PROJ_COLS), BF16), jax.ShapeDtypeStruct((m, LANES), F32)],
        scratch_shapes=[pltpu.VMEM((rt, D_MODEL), BF16)],
        compiler_params=_cparams(("arbitrary", "arbitrary")),
        name="in_proj",
    )(x2d, norm_w, w_cat, w_dt)


CONV_KEEP = 2 * SUBLANES
CONV_SEC = 256


def _seg_cumsum_rows(x, seg):
    t = lax.broadcasted_iota(jnp.int32, x.shape, 0) & (seg - 1)
    s = 1
    while s < seg:
        x = x + jnp.where(t >= s, pltpu.roll(x, s, axis=0), 0.0)
        s *= 2
    return x


def _seg_revsum_rows(x, seg):
    n = x.shape[0]
    t = lax.broadcasted_iota(jnp.int32, x.shape, 0) & (seg - 1)
    s = 1
    while s < seg:
        x = x + jnp.where(t + s < seg, pltpu.roll(x, n - s, axis=0), 0.0)
        s *= 2
    return x


def _to_channel_major(src, dst):
    for j in range(SSD_WIDTH // LANES):
        sl = slice(j * LANES, (j + 1) * LANES)
        dst[sl, :] = src[:, sl].astype(F32).T


def _gate_norm_store(yt, zt, nw_ref, yz_ref):
    for g in range(SSD_GROUPS):
        sl = slice(g * SSD_GW, (g + 1) * SSD_GW)
        v = yt[sl, :] * _silu(zt[sl, :])
        ms = jnp.mean(v * v, axis=0, keepdims=True)
        yt[sl, :] = v * lax.rsqrt(ms + NORM_EPS) * nw_ref[sl, :]
    for j in range(SSD_WIDTH // LANES):
        sl = slice(j * LANES, (j + 1) * LANES)
        yz_ref[:, sl] = yt[sl, :].T.astype(yz_ref.dtype)


def _group_bc(act, g):
    bg = act[:, SSD_WIDTH + g * SSD_STATE:SSD_WIDTH + (g + 1) * SSD_STATE].astype(BF16)
    cg = act[:, SSD_WIDTH + 1024 + g * SSD_STATE:SSD_WIDTH + 1024 + (g + 1) * SSD_STATE].astype(BF16)
    return bg, cg


def _ssd_body(z_ref, xs_ref, b_ref, c_ref, dt_ref, cw_ref, cb_ref, dtb_ref, alog_ref, dexp_ref, nw_ref,
              yz_ref, ssmo_ref, cbuf, act, xt, zt, yt, state):
    q = SSD_CHUNK
    hp = SSD_HEAD_DIM

    @pl.when(pl.program_id(1) == 0)
    def _():
        state[...] = jnp.zeros_like(state)
        cbuf[q:, :] = jnp.zeros((q, SSD_CONV_DIM), BF16)

    cbuf[:q, 0:SSD_WIDTH] = xs_ref[...]
    cbuf[:q, SSD_WIDTH:SSD_WIDTH + 1024] = b_ref[...]
    cbuf[:q, SSD_WIDTH + 1024:] = c_ref[...]
    t_i = lax.broadcasted_iota(jnp.int32, (q, 2 * q), 0)
    r_i = lax.broadcasted_iota(jnp.int32, (q, 2 * q), 1)
    shifts = [jnp.where(r_i == ((t_i - d) & (2 * q - 1)), 1.0, 0.0).astype(BF16) for d in range(1, SSD_CONV)]
    for s0 in range(0, SSD_CONV_DIM, CONV_SEC):
        cs = slice(s0, s0 + CONV_SEC)
        acc = cb_ref[:, cs] + cw_ref[SSD_CONV - 1:SSD_CONV, cs] * cbuf[:q, cs].astype(F32)
        both = cbuf[:, cs]
        for d in range(1, SSD_CONV):
            acc = acc + cw_ref[SSD_CONV - 1 - d:SSD_CONV - d, cs] * _dot(shifts[d - 1], both)
        act[:, cs] = _silu(acc)
    cbuf[2 * q - CONV_KEEP:, :] = cbuf[q - CONV_KEEP:q, :]

    dt = jax.nn.softplus(dt_ref[...] + dtb_ref[...])
    a_cs = _seg_cumsum_rows(dt * (-jnp.exp(alog_ref[...])), q) * LOG2E
    a_cst = a_cs.T
    dtt = dt.T
    _to_channel_major(act, xt)
    _to_channel_major(z_ref, zt)

    causal = lax.broadcasted_iota(jnp.int32, (q, q), 1) >= lax.broadcasted_iota(jnp.int32, (q, q), 0)
    for g in range(SSD_GROUPS):
        bg, cg = _group_bc(act, g)
        cbt = _dot_nt(bg, cg)
        stg = state[g * SSD_GW:(g + 1) * SSD_GW, :]
        yoff = _dot_nt(stg.astype(BF16), cg)
        for r in range(SSD_HPG):
            h = g * SSD_HPG + r
            hs = slice(h * hp, (h + 1) * hp)
            arow = a_cst[h:h + 1, :]
            acol = a_cs[:, h:h + 1]
            lt = jnp.where(causal, jnp.exp2(arow - acol), 0.0)
            wt = (cbt * lt).astype(BF16)
            xh = xt[hs, :]
            xdt = xh * dtt[h:h + 1, :]
            yt[hs, :] = (_dot(xdt.astype(BF16), wt) + yoff[r * hp:(r + 1) * hp, :] * jnp.exp2(arow)
                         + dexp_ref[hs, :] * xh)
            atot = arow[:, q - 1:q]
            xdec = xdt * jnp.exp2(atot - arow)
            state[hs, :] = jnp.exp2(atot) * stg[r * hp:(r + 1) * hp, :] + _dot(xdec.astype(BF16), bg)

    ssmo_ref[...] = state[...]
    _gate_norm_store(yt, zt, nw_ref, yz_ref)


def _ssd_consts(conv_w, conv_b, dt_bias, a_log, d_ssd, ssd_norm_w):
    hrep = jnp.broadcast_to(jnp.repeat(d_ssd, SSD_HEAD_DIM)[:, None], (SSD_WIDTH, LANES))
    padl = lambda v: jnp.pad(v, (0, LANES - SSD_HEADS)).reshape(1, LANES)
    return (conv_w, conv_b.reshape(1, -1), padl(dt_bias), padl(a_log), hrep,
            jnp.broadcast_to(ssd_norm_w[:, None], (SSD_WIDTH, LANES)))


def _ssd_const_specs(n_grid):
    zeros = (0,) * 2
    const = lambda shape: pl.BlockSpec(shape, lambda *_: zeros)
    return [const((SSD_CONV, SSD_CONV_DIM)), const((1, SSD_CONV_DIM)), const((1, LANES)), const((1, LANES)),
            const((SSD_WIDTH, LANES)), const((SSD_WIDTH, LANES))]


def _ssd_proj_specs(rows, rowf):
    return [pl.BlockSpec((rows, SSD_WIDTH), lambda *a: (rowf(*a), 0)),
            pl.BlockSpec((rows, SSD_WIDTH), lambda *a: (rowf(*a), 1)),
            pl.BlockSpec((rows, 1024), lambda *a: (rowf(*a), CB_B)),
            pl.BlockSpec((rows, 1024), lambda *a: (rowf(*a), CB_C)),
            pl.BlockSpec((rows, LANES), lambda *a: (rowf(*a), 0))]


def _ssd(proj, dt, nseq, nchunk, consts):
    m = proj.shape[0]
    rowf = lambda b, c: b * nchunk + c
    chan = pltpu.VMEM((SSD_WIDTH, LANES), F32)
    return pl.pallas_call(
        _ssd_body,
        grid=(nseq, nchunk),
        in_specs=_ssd_proj_specs(SSD_CHUNK, rowf) + _ssd_const_specs(2),
        out_specs=[pl.BlockSpec((SSD_CHUNK, SSD_WIDTH), lambda b, c: (rowf(b, c), 0)),
                   pl.BlockSpec((None, SSD_WIDTH, SSD_STATE), lambda b, c: (b, 0, 0))],
        out_shape=[jax.ShapeDtypeStruct((m, SSD_WIDTH), BF16),
                   jax.ShapeDtypeStruct((nseq, SSD_WIDTH, SSD_STATE), F32)],
        scratch_shapes=[pltpu.VMEM((2 * SSD_CHUNK, SSD_CONV_DIM), BF16),
                        pltpu.VMEM((SSD_CHUNK, SSD_CONV_DIM), F32),
                        chan, chan, chan, pltpu.VMEM((SSD_WIDTH, SSD_STATE), F32)],
        compiler_params=_cparams(("arbitrary", "arbitrary")),
        name="ssd",
    )(proj, proj, proj, proj, dt, *consts)


SSD_PACK = 16
SSD_TOK = SSD_CHUNK // SSD_PACK
SSD_SUB = 4


def _ssd_decay_body(dt_ref, dtb_ref, alog_ref, o_ref):
    da = jax.nn.softplus(dt_ref[...] + dtb_ref[...]) * (-jnp.exp(alog_ref[...]))
    o_ref[...] = jnp.exp(jnp.sum(da.reshape(o_ref.shape[0], SSD_TOK, LANES), axis=1))


def _ssd_decay(dt, nseq, dtb, alog):
    const = lambda shape: pl.BlockSpec(shape, lambda i: (0, 0))
    return pl.pallas_call(
        _ssd_decay_body,
        grid=(1,),
        in_specs=[const((nseq * SSD_TOK, LANES)), const((1, LANES)), const((1, LANES))],
        out_specs=const((nseq, LANES)),
        out_shape=jax.ShapeDtypeStruct((nseq, LANES), F32),
        compiler_params=_cparams(("arbitrary",)),
        name="ssd_decay",
    )(dt, dtb, alog)


def _ssd_packed_body(dec_ref, z_ref, xs_ref, b_ref, c_ref, dt_ref, cw_ref, cb_ref, dtb_ref, alog_ref,
                     dexp_ref, nw_ref, convp_ref, ssm0_ref,
                     yz_ref, ssmo_ref, act, xt, zt, yt, xd, ear, bgs, cgs):
    q = SSD_CHUNK
    tl = SSD_TOK
    hp = SSD_HEAD_DIM
    tile = pl.program_id(0)
    sub = pl.program_id(1)

    @pl.when(sub == 0)
    def _shared():
        rowt = lax.broadcasted_iota(jnp.int32, (q, CONV_SEC), 0) & (tl - 1)
        for s0 in range(0, SSD_CONV_DIM, CONV_SEC):
            cs = slice(s0, s0 + CONV_SEC)
            if s0 < SSD_WIDTH:
                x = xs_ref[:, cs]
            elif s0 < SSD_WIDTH + 1024:
                x = b_ref[:, s0 - SSD_WIDTH:s0 - SSD_WIDTH + CONV_SEC]
            else:
                x = c_ref[:, s0 - SSD_WIDTH - 1024:s0 - SSD_WIDTH - 1024 + CONV_SEC]
            x = x.astype(F32)
            e = convp_ref[:, :, cs].reshape(q, CONV_SEC)
            acc = cb_ref[:, cs] + cw_ref[SSD_CONV - 1:SSD_CONV, cs] * x
            for d in range(1, SSD_CONV):
                prev = jnp.where(rowt >= d, pltpu.roll(x, d, axis=0), pltpu.roll(e, q - tl + d, axis=0))
                acc = acc + cw_ref[SSD_CONV - 1 - d:SSD_CONV - d, cs] * prev
            act[:, cs] = _silu(acc)

        dt = jax.nn.softplus(dt_ref[...] + dtb_ref[...])
        da = dt * (-jnp.exp(alog_ref[...]))
        a_cs = _seg_cumsum_rows(da, tl)
        a_tot = a_cs + _seg_revsum_rows(da, tl) - da
        a_cst, a_tott, dtt = a_cs.T, a_tot.T, dt.T
        ea = jnp.exp(a_cst)
        _to_channel_major(act, xt)
        _to_channel_major(z_ref, zt)

        s_i = lax.broadcasted_iota(jnp.int32, (q, q), 0)
        l_i = lax.broadcasted_iota(jnp.int32, (q, q), 1)
        causal = (l_i >= s_i) & (jnp.right_shift(l_i, 3) == jnp.right_shift(s_i, 3))
        for g in range(SSD_GROUPS):
            bg, cg = _group_bc(act, g)
            bgs[g] = bg
            cgs[g] = cg
            cbt = _dot_nt(bg, cg)
            for r in range(SSD_HPG):
                h = g * SSD_HPG + r
                hs = slice(h * hp, (h + 1) * hp)
                arow = a_cst[h:h + 1, :]
                acol = a_cs[:, h:h + 1]
                lt = jnp.where(causal, jnp.exp(arow - acol), 0.0)
                wt = (cbt * lt).astype(BF16)
                xh = xt[hs, :]
                xdt = xh * dtt[h:h + 1, :]
                yt[hs, :] = _dot(xdt.astype(BF16), wt) + dexp_ref[hs, :] * xh
                xd[hs, :] = xdt * jnp.exp(a_tott[h:h + 1, :] - arow)
                ear[hs, :] = jnp.broadcast_to(ea[h:h + 1, :], (hp, q))

    lane_seq = jnp.right_shift(lax.broadcasted_iota(jnp.int32, (SSD_GW, q), 1), 3)

    def seq_body(sl, carry):
        sq = sub * SSD_SUB + sl
        inseq = lane_seq == sq
        dbase = (tile * SSD_PACK + sq) * SSD_HEADS
        for g in range(SSD_GROUPS):
            rows = slice(g * SSD_GW, (g + 1) * SSD_GW)
            stg = ssm0_ref[sl, rows, :]
            yo = _dot_nt(stg.astype(BF16), cgs[g])
            yt[rows, :] = yt[rows, :] + jnp.where(inseq, yo * ear[rows, :], 0.0)
            upd = _dot(jnp.where(inseq, xd[rows, :], 0.0).astype(BF16), bgs[g])
            for r in range(SSD_HPG):
                h = g * SSD_HPG + r
                ssmo_ref[sl, h * hp:(h + 1) * hp, :] = (dec_ref[dbase + h] * stg[r * hp:(r + 1) * hp, :]
                                                        + upd[r * hp:(r + 1) * hp, :])
        return carry

    lax.fori_loop(0, SSD_SUB, seq_body, 0)

    @pl.when(sub == SSD_PACK // SSD_SUB - 1)
    def _():
        _gate_norm_store(yt, zt, nw_ref, yz_ref)


def _ssd_packed(proj, dt, nseq, consts, conv0, ssm0):
    m = proj.shape[0]
    ntile = nseq // SSD_PACK
    nsub = SSD_PACK // SSD_SUB
    dec = _ssd_decay(dt, nseq, consts[2], consts[3])[:, :SSD_HEADS].reshape(-1)
    convp = jnp.pad(conv0, ((0, 0), (SSD_TOK - (SSD_CONV - 1), 0), (0, 0)))
    chan = pltpu.VMEM((SSD_WIDTH, LANES), F32)
    st_spec = pl.BlockSpec((SSD_SUB, SSD_WIDTH, SSD_STATE), lambda i, s: (i * nsub + s, 0, 0))
    return pl.pallas_call(
        _ssd_packed_body,
        grid=(ntile, nsub),
        in_specs=[pl.BlockSpec(memory_space=pltpu.SMEM)]
        + _ssd_proj_specs(SSD_CHUNK, lambda i, s: i) + _ssd_const_specs(2)
        + [pl.BlockSpec((SSD_PACK, SSD_TOK, SSD_CONV_DIM), lambda i, s: (i, 0, 0)), st_spec],
        out_specs=[pl.BlockSpec((SSD_CHUNK, SSD_WIDTH), lambda i, s: (i, 0)), st_spec],
        out_shape=[jax.ShapeDtypeStruct((m, SSD_WIDTH), BF16),
                   jax.ShapeDtypeStruct((nseq, SSD_WIDTH, SSD_STATE), F32)],
        scratch_shapes=[pltpu.VMEM((SSD_CHUNK, SSD_CONV_DIM), F32), chan, chan, chan, chan, chan,
                        pltpu.VMEM((SSD_GROUPS, SSD_CHUNK, SSD_STATE), BF16),
                        pltpu.VMEM((SSD_GROUPS, SSD_CHUNK, SSD_STATE), BF16)],
        compiler_params=_cparams(("arbitrary", "arbitrary")),
        name="ssd_packed",
    )(dec, proj, proj, proj, proj, dt, *consts, convp, ssm0)


S5_GB = 8
S5_SCAN_SEQS = 2


def _s5_body(*refs, q, scan):
    if scan:
        u_ref, toep_ref, bin_ref, coff_ref, aqr_ref, aqi_ref, y_ref, so_ref, sr_s, si_s = refs
        s0_ref = None
    else:
        u_ref, toep_ref, bin_ref, coff_ref, aqr_ref, aqi_ref, s0_ref, y_ref, so_ref = refs
    k = S5_GROUP_SIZE
    p = S5_STATE
    gp = S5_GB * p
    w = u_ref.shape[2]
    nsq = w // LANES

    def chunk_in(gl):
        return u_ref[:, gl * k:(gl + 1) * k, :].reshape(q * k, w)

    if scan:
        for gl in range(S5_GB):
            bo = _dot(bin_ref[gl], chunk_in(gl).astype(BF16))
            sr_s[gl * p:(gl + 1) * p, :] = bo[:p, :]
            si_s[gl * p:(gl + 1) * p, :] = bo[p:, :]
        lane = lax.broadcasted_iota(jnp.int32, (gp, w), 1) & (LANES - 1)
        xr, xi = sr_s[...], si_s[...]
        pr = jnp.concatenate([aqr_ref[...].reshape(gp, LANES)] * nsq, axis=1)
        pi = jnp.concatenate([aqi_ref[...].reshape(gp, LANES)] * nsq, axis=1)
        s = 1
        while s < LANES:
            m = lane >= s
            tr = jnp.where(m, pltpu.roll(xr, s, axis=1), 0.0)
            ti = jnp.where(m, pltpu.roll(xi, s, axis=1), 0.0)
            xr, xi = xr + (pr * tr - pi * ti), xi + (pr * ti + pi * tr)
            if 2 * s < LANES:
                pr, pi = pr * pr - pi * pi, 2.0 * pr * pi
            s *= 2
        for sq in range(nsq):
            last = slice((sq + 1) * LANES - 1, (sq + 1) * LANES)
            so_ref[sq, :, :p, :] = xr[:, last].reshape(S5_GB, p, 1)
            so_ref[sq, :, p:, :] = xi[:, last].reshape(S5_GB, p, 1)
        m1 = lane >= 1
        sr_s[...] = jnp.where(m1, pltpu.roll(xr, 1, axis=1), 0.0)
        si_s[...] = jnp.where(m1, pltpu.roll(xi, 1, axis=1), 0.0)

    for gl in range(S5_GB):
        z = chunk_in(gl)
        if scan:
            sin_r, sin_i = sr_s[gl * p:(gl + 1) * p, :], si_s[gl * p:(gl + 1) * p, :]
        else:
            bo = _dot_hi(bin_ref[gl], z)
            aqr, aqi = aqr_ref[gl], aqi_ref[gl]
            sin_r, sin_i = s0_ref[gl, :p, :], s0_ref[gl, p:, :]
            so_ref[gl, :p, :] = aqr * sin_r - aqi * sin_i + bo[:p, :]
            so_ref[gl, p:, :] = aqr * sin_i + aqi * sin_r + bo[p:, :]
        s_in = jnp.concatenate([sin_r, sin_i], axis=0).astype(BF16)
        y = _dot(toep_ref[gl], z.astype(BF16)) + _dot(coff_ref[gl], s_in)
        y_ref[:, gl * k:(gl + 1) * k, :] = y.reshape(q, k, w).astype(y_ref.dtype)


def _s5(ut, q, scan, prep, s0t=None):
    ctot = ut.shape[2]
    lw = min(S5_SCAN_SEQS * LANES, ctot) if scan else LANES
    nlb = ctot // lw
    ngb = S5_GROUPS // S5_GB
    qk = q * S5_GROUP_SIZE
    gspec = lambda r, c, cb=0: pl.BlockSpec((S5_GB, r, c), lambda gb, lb: (gb, 0, cb))
    in_specs = [pl.BlockSpec((q, S5_GB * S5_GROUP_SIZE, lw), lambda gb, lb: (0, gb, lb)),
                gspec(qk, qk), gspec(2 * S5_STATE, qk, (S5_Q - q) * S5_GROUP_SIZE // qk),
                gspec(qk, 2 * S5_STATE), gspec(S5_STATE, LANES), gspec(S5_STATE, LANES)]
    args = [ut, prep["toep"], prep["binb"] if scan else prep["bin"], prep["coff"], *prep["aq"][q]]
    scratch = []
    if scan:
        so_shape = jax.ShapeDtypeStruct((ctot // LANES, S5_GROUPS, 2 * S5_STATE, 1), F32)
        so_spec = pl.BlockSpec((lw // LANES, S5_GB, 2 * S5_STATE, 1), lambda gb, lb: (lb, gb, 0, 0))
        scratch = [pltpu.VMEM((S5_GB * S5_STATE, lw), F32)] * 2
    else:
        in_specs.append(gspec(2 * S5_STATE, LANES))
        args.append(s0t)
        so_shape = jax.ShapeDtypeStruct((S5_GROUPS, 2 * S5_STATE, LANES), F32)
        so_spec = gspec(2 * S5_STATE, LANES)
    return pl.pallas_call(
        functools.partial(_s5_body, q=q, scan=scan),
        grid=(ngb, nlb),
        in_specs=in_specs,
        out_specs=[pl.BlockSpec((q, S5_GB * S5_GROUP_SIZE, lw), lambda gb, lb: (0, gb, lb)), so_spec],
        out_shape=[jax.ShapeDtypeStruct(ut.shape, BF16), so_shape],
        scratch_shapes=scratch,
        compiler_params=_cparams(("arbitrary", "arbitrary")),
        name="s5_scan" if scan else "s5_step",
    )(*args)


OUT_ROWS = 512


def _out_body(y5_ref, u5_ref, z5_ref, ga_ref, gb_ref, yz_ref, x_ref, d5_ref, wglu_ref, bglu_ref,
              wd5_ref, wda_ref, wout_ref, fnw_ref, o_ref):
    f32 = lambda r: r[...].astype(F32)
    y5 = jax.nn.gelu(f32(y5_ref) + d5_ref[...] * f32(u5_ref))
    y5 = y5 * _sigmoid(_dot(y5.astype(BF16), wglu_ref[...]) + bglu_ref[...])
    y5 = y5 * _silu(f32(z5_ref))
    y_b = _dot(y5.astype(BF16), wd5_ref[...])
    y_a = _dot(yz_ref[...], wda_ref[...])
    mix = _sigmoid(f32(ga_ref)) * y_a + _sigmoid(f32(gb_ref)) * y_b
    out = x_ref[...] + _dot(mix.astype(BF16), wout_ref[...])
    ms = jnp.mean(out * out, axis=-1, keepdims=True)
    o_ref[...] = out * lax.rsqrt(ms + NORM_EPS) * fnw_ref[...]


def _out(y5, proj, yz, x2d, prm):
    m = x2d.shape[0]
    rowb = lambda w, cb: pl.BlockSpec((OUT_ROWS, w), lambda i: (i, cb))
    const = lambda r, c: pl.BlockSpec((r, c), lambda i: (0, 0))
    return pl.pallas_call(
        _out_body,
        grid=(m // OUT_ROWS,),
        in_specs=[rowb(1024, 0), rowb(1024, CB_U5), rowb(1024, CB_Z5), rowb(1024, CB_GA), rowb(1024, CB_GB),
                  rowb(SSD_WIDTH, 0), rowb(1024, 0),
                  const(1, 1024), const(1024, 1024), const(1, 1024), const(1024, 1024),
                  const(SSD_WIDTH, 1024), const(1024, 1024), const(1, 1024)],
        out_specs=rowb(1024, 0),
        out_shape=jax.ShapeDtypeStruct((m, D_MODEL), F32),
        compiler_params=_cparams(("arbitrary",)),
        name="out_stage",
    )(y5, proj, proj, proj, proj, yz, x2d, prm["d_s5"], prm["w_glu"], prm["b_glu"],
      prm["w_down_s5"], prm["w_down_ssd"], prm["w_out"], prm["final_norm_w"])


def _run(x, conv0, ssm0, re0, im0, prm):
    nseq, seqlen, _ = x.shape
    m = nseq * seqlen
    prompt = conv0 is None
    x2d = x.reshape(m, D_MODEL)
    proj, dt = _in_proj(x2d, prm["norm_w"], prm["w_cat"], prm["w_dt"])
    conv_new = proj.reshape(nseq, seqlen, PROJ_COLS)[:, seqlen - (SSD_CONV - 1):,
                                                     SSD_WIDTH:SSD_WIDTH + SSD_CONV_DIM].astype(F32)

    if prompt:
        yz, ssm_new = _ssd(proj, dt, nseq, seqlen // SSD_CHUNK, prm["ssd_consts"])
    else:
        yz, ssm_new = _ssd_packed(proj, dt, nseq, prm["ssd_consts"], conv0,
                                  ssm0.reshape(nseq, SSD_WIDTH, SSD_STATE))

    q = S5_Q if prompt else seqlen
    u5 = proj[:, CB_U5 * 1024:(CB_U5 + 1) * 1024]
    ut = jnp.transpose(u5.reshape(m // q, q, S5_WIDTH), (1, 2, 0))
    if prompt:
        yt, so = _s5(ut, q, True, prm["s5"])
        so = so.reshape(nseq, S5_GROUPS, 2, S5_STATE)
        re_new, im_new = so[:, :, 0, :], so[:, :, 1, :]
    else:
        s0t = jnp.transpose(jnp.concatenate([re0, im0], axis=-1), (1, 2, 0))
        yt, so = _s5(ut, q, False, prm["s5"], s0t)
        so = jnp.transpose(so, (2, 0, 1))
        re_new, im_new = so[:, :, :S5_STATE], so[:, :, S5_STATE:]
    y5 = jnp.transpose(yt, (2, 0, 1)).reshape(m, S5_WIDTH)

    out = _out(y5, proj, yz, x2d, prm)
    return (out.reshape(nseq, seqlen, D_MODEL), conv_new[None],
            ssm_new.reshape(1, nseq, SSD_HEADS, SSD_HEAD_DIM, SSD_STATE), re_new[None], im_new[None])


def kernel(x_prompt, x_sample, state_conv, state_ssm, state_s5_re, state_s5_im, norm_w, w_in, conv_w, conv_b,
           dt_bias, A_log, D_ssd, ssd_norm_w, w_down_ssd, lam_re, lam_im, log_dt, B_re, B_im, C_re, C_im, D_s5,
           w_glu, b_glu, w_down_s5, w_out, final_norm_w):
    assert norm_w.shape[0] == 1, "single-layer trunk"
    assert x_prompt.shape[1] % SSD_CHUNK == 0 and x_prompt.shape[1] // S5_Q == LANES
    assert x_sample.shape[1] == SSD_TOK and x_sample.shape[0] == LANES
    w = w_in[0]
    o_dt = SSD_WIDTH + SSD_CONV_DIM
    w_cat = jnp.concatenate([w[:, :o_dt], w[:, o_dt + SSD_HEADS:]], axis=1).astype(BF16)
    w_dt = jnp.pad(w[:, o_dt:o_dt + SSD_HEADS], ((0, 0), (0, LANES - SSD_HEADS))).astype(BF16)
    prm = dict(norm_w=norm_w[0].reshape(1, -1), w_cat=w_cat, w_dt=w_dt,
               ssd_consts=_ssd_consts(conv_w[0], conv_b[0], dt_bias[0], A_log[0], D_ssd[0], ssd_norm_w[0]),
               s5=_s5_prep(lam_re[0], lam_im[0], log_dt[0], B_re[0], B_im[0], C_re[0], C_im[0]),
               d_s5=D_s5[0].reshape(1, -1), w_glu=w_glu[0].astype(BF16), b_glu=b_glu[0].reshape(1, -1),
               w_down_s5=w_down_s5[0].astype(BF16), w_down_ssd=w_down_ssd[0].astype(BF16),
               w_out=w_out[0].astype(BF16), final_norm_w=final_norm_w.reshape(1, -1))

    y_p, conv_p, ssm_p, re_p, im_p = _run(x_prompt, None, None, None, None, prm)
    y_s, conv_s, ssm_s, re_s, im_s = _run(x_sample, state_conv[0], state_ssm[0], state_s5_re[0],
                                          state_s5_im[0], prm)
    return (y_p, y_s, conv_p, ssm_p, re_p, im_p, conv_s, ssm_s, re_s, im_s)
```

```python
import functools

import jax
import jax.numpy as jnp
from jax import lax
from jax.experimental import pallas as pl
from jax.experimental.pallas import tpu as pltpu

F32 = jnp.float32
BF16 = jnp.bfloat16

D_MODEL = 1024
SSD_WIDTH = 2048
SSD_HEAD_DIM = 64
SSD_HEADS = 32
SSD_GROUPS = 8
SSD_HPG = SSD_HEADS // SSD_GROUPS
SSD_GW = SSD_WIDTH // SSD_GROUPS
SSD_STATE = 128
SSD_CONV = 4
SSD_CHUNK = 128
SSD_CONV_DIM = 4096
S5_WIDTH = 1024
S5_GROUP_SIZE = 16
S5_GROUPS = 64
S5_STATE = 64
NORM_EPS = 1e-6
LOG2E = 1.4426950408889634

LANES = 128
SUBLANES = 8

PROJ_COLS = 10240
COL_TILE = 1024
ROW_TILE = 2048
CB_B, CB_C, CB_U5, CB_Z5, CB_GA, CB_GB = 4, 5, 6, 7, 8, 9

VMEM_LIMIT = 56 * 1024 * 1024


def _cparams(sem):
    return pltpu.CompilerParams(dimension_semantics=sem, vmem_limit_bytes=VMEM_LIMIT)


def _sigmoid(x):
    return 0.5 * jnp.tanh(0.5 * x) + 0.5


def _silu(x):
    return x * _sigmoid(x)


def _dot(a, b):
    return jnp.dot(a, b, preferred_element_type=F32)


def _dot_nt(a, b):
    return lax.dot_general(a, b, (((1,), (1,)), ((), ())), preferred_element_type=F32)


def _dot_hi(a, b):
    a_hi = a.astype(BF16)
    a_lo = (a - a_hi.astype(F32)).astype(BF16)
    if b.dtype == BF16:
        return _dot(a_hi, b) + _dot(a_lo, b)
    b_hi = b.astype(BF16)
    b_lo = (b - b_hi.astype(F32)).astype(BF16)
    return _dot(a_hi, b_hi) + _dot(a_hi, b_lo) + _dot(a_lo, b_hi)


S5_Q = 16
PREP_GB = 8


def _zoh(lr, li, ldt):
    step = jnp.exp(ldt)
    mag = jnp.exp(lr * step)
    return mag * jnp.cos(li * step), mag * jnp.sin(li * step)


def _s5_prep_body(rowp_ref, cre_ref, cim_ref, b1_ref, b2_ref,
                  toep_ref, coff_ref, bin_ref, binb_ref, aq16r_ref, aq16i_ref, aq8r_ref, aq8i_ref, ca_s):
    k, p, q = S5_GROUP_SIZE, S5_STATE, S5_Q
    qk = q * k
    lr, li = rowp_ref[:, 0:1, :], rowp_ref[:, 1:2, :]
    ar_l, ai_l = _zoh(lr, li, rowp_ref[:, 2:3, :])
    den = lr * lr + li * li
    nr = ar_l - 1.0
    cr_l = (nr * lr + ai_l * li) / den
    ci_l = (ai_l * lr - nr * li) / den

    def on_sublanes(v):
        return jnp.stack([jnp.broadcast_to(v[g], (2 * p, 2 * p)).T for g in range(PREP_GB)])

    wide = lambda v: jnp.concatenate([v] * (qk // (2 * p)), axis=-1)

    ar, ai, cr, ci = on_sublanes(ar_l), on_sublanes(ai_l), on_sublanes(cr_l), on_sublanes(ci_l)
    sgn_row = jnp.where(lax.broadcasted_iota(jnp.int32, (PREP_GB, 2 * p, qk), 1) < p, -1.0, 1.0)
    b1, b2 = b1_ref[...], b2_ref[...]
    crw, ciw = wide(cr), sgn_row * wide(ci)
    bb = crw * b1 + ciw * b2
    bb_sw = crw * b2 - ciw * b1
    lane_t = jnp.right_shift(lax.broadcasted_iota(jnp.int32, (PREP_GB, 2 * p, qk), 2), 4)
    pr, pi = jnp.ones_like(ar), jnp.zeros_like(ar)
    pwr = jnp.zeros((PREP_GB, 2 * p, qk), F32)
    pwi = jnp.zeros((PREP_GB, 2 * p, qk), F32)
    for j in range(q + 1):
        if j < q:
            sel = lane_t == (q - 1 - j)
            pwr = jnp.where(sel, wide(pr), pwr)
            pwi = jnp.where(sel, wide(pi), pwi)
        if j == q // 2:
            aq8r_ref[...] = pr[:, :p, :]
            aq8i_ref[...] = pi[:, :p, :]
        if j == q:
            aq16r_ref[...] = pr[:, :p, :]
            aq16i_ref[...] = pi[:, :p, :]
        pr, pi = pr * ar - pi * ai, pr * ai + pi * ar
    binv = pwr * bb + sgn_row * pwi * bb_sw
    bin_ref[...] = binv
    binb_ref[...] = binv.astype(BF16)

    ar, ai = ar_l, ai_l
    low = lax.broadcasted_iota(jnp.int32, (PREP_GB, 1, 2 * p), 2) < p
    sgn_lane = jnp.where(low, 1.0, -1.0)
    cre, cim = cre_ref[...], cim_ref[...]
    pr, pi = jnp.ones_like(ar), jnp.zeros_like(ar)
    for j in range(q + 1):
        p1 = jnp.where(low, pr, pi)
        p2 = jnp.where(low, pi, pr)
        ca_s[:, j * k:(j + 1) * k, :] = sgn_lane * cre * p1 - cim * p2
        pr, pi = pr * ar - pi * ai, pr * ai + pi * ar
    coff_ref[...] = ca_s[:, k:, :].astype(BF16)

    lane_blk = jnp.right_shift(lax.broadcasted_iota(jnp.int32, (qk, qk), 1), 4)
    for g in range(PREP_GB):
        km = _dot_hi(ca_s[g, :qk, :], bb[g])
        tp = jnp.where(lane_blk == 0, km, 0.0)
        for t in range(1, q):
            sh = jnp.concatenate([jnp.zeros((t * k, qk), F32), km[:qk - t * k, :]], axis=0)
            tp = jnp.where(lane_blk == t, sh, tp)
        toep_ref[g] = tp.astype(BF16)


def _s5_prep(lam_re, lam_im, log_dt, b_re, b_im, c_re, c_im):
    g, p, k, q = S5_GROUPS, S5_STATE, S5_GROUP_SIZE, S5_Q
    qk = q * k
    dup = lambda a: jnp.concatenate([a, a], axis=-1)
    ldt = jnp.broadcast_to(log_dt[:, None], (g, 2 * p))
    rowp = jnp.stack([dup(lam_re), dup(lam_im), ldt], axis=1)
    b1 = jnp.tile(jnp.concatenate([b_re, b_im], axis=1), (1, 1, q))
    b2 = jnp.tile(jnp.concatenate([b_im, b_re], axis=1), (1, 1, q))
    blk = lambda r, c: pl.BlockSpec((PREP_GB, r, c), lambda i: (i, 0, 0))
    shp = lambda r, c, dt: jax.ShapeDtypeStruct((g, r, c), dt)
    toep, coff, bin_, binb, aq16r, aq16i, aq8r, aq8i = pl.pallas_call(
        _s5_prep_body,
        grid=(g // PREP_GB,),
        in_specs=[blk(3, 2 * p), blk(k, 2 * p), blk(k, 2 * p), blk(2 * p, qk), blk(2 * p, qk)],
        out_specs=[blk(qk, qk), blk(qk, 2 * p), blk(2 * p, qk), blk(2 * p, qk),
                   blk(p, LANES), blk(p, LANES), blk(p, LANES), blk(p, LANES)],
        out_shape=[shp(qk, qk, BF16), shp(qk, 2 * p, BF16), shp(2 * p, qk, F32), shp(2 * p, qk, BF16),
                   shp(p, LANES, F32), shp(p, LANES, F32), shp(p, LANES, F32), shp(p, LANES, F32)],
        scratch_shapes=[pltpu.VMEM((PREP_GB, (q + 1) * k, 2 * p), F32)],
        compiler_params=_cparams(("arbitrary",)),
        name="s5_prep",
    )(rowp, dup(c_re), dup(c_im), b1, b2)
    return dict(toep=toep, coff=coff, bin=bin_, binb=binb,
                aq={16: (aq16r, aq16i), 8: (aq8r, aq8i)})


def _in_proj_body(x_ref, nw_ref, w_ref, wdt_ref, o_ref, dt_ref, u5_ref, h_ref):
    j = pl.program_id(1)

    @pl.when(j == 0)
    def _():
        x = x_ref[...]
        ms = jnp.mean(x * x, axis=-1, keepdims=True)
        h = (x * lax.rsqrt(ms + NORM_EPS) * nw_ref[...]).astype(BF16)
        h_ref[...] = h
        dt_ref[...] = _dot(h, wdt_ref[...])

    o = _dot(h_ref[...], w_ref[...]).astype(BF16)
    o_ref[...] = o

    @pl.when(j == CB_U5)
    def _():
        u5_ref[...] = o


def _in_proj(x2d, norm_w, w_cat, w_dt):
    m = x2d.shape[0]
    rt = min(ROW_TILE, m)
    return pl.pallas_call(
        _in_proj_body,
        grid=(m // rt, PROJ_COLS // COL_TILE),
        in_specs=[pl.BlockSpec((rt, D_MODEL), lambda i, j: (i, 0)),
                  pl.BlockSpec((1, D_MODEL), lambda i, j: (0, 0)),
                  pl.BlockSpec((D_MODEL, COL_TILE), lambda i, j: (0, j)),
                  pl.BlockSpec((D_MODEL, LANES), lambda i, j: (0, 0))],
        out_specs=[pl.BlockSpec((rt, COL_TILE), lambda i, j: (i, j)),
                   pl.BlockSpec((rt, LANES), lambda i, j: (i, 0)),
                   pl.BlockSpec((rt, COL_TILE), lambda i, j: (i, 0))],
        out_shape=[jax.ShapeDtypeStruct((m, PROJ_COLS), BF16), jax.ShapeDtypeStruct((m, LANES), F32),
                   jax.ShapeDtypeStruct((m, S5_WIDTH), BF16)],
        scratch_shapes=[pltpu.VMEM((rt, D_MODEL), BF16)],
        compiler_params=_cparams(("arbitrary", "arbitrary")),
        name="in_proj",
    )(x2d, norm_w, w_cat, w_dt)


CONV_KEEP = 2 * SUBLANES
CONV_SEC = 256


def _seg_cumsum_rows(x, seg):
    t = lax.broadcasted_iota(jnp.int32, x.shape, 0) & (seg - 1)
    s = 1
    while s < seg:
        x = x + jnp.where(t >= s, pltpu.roll(x, s, axis=0), 0.0)
        s *= 2
    return x


def _seg_revsum_rows(x, seg):
    n = x.shape[0]
    t = lax.broadcasted_iota(jnp.int32, x.shape, 0) & (seg - 1)
    s = 1
    while s < seg:
        x = x + jnp.where(t + s < seg, pltpu.roll(x, n - s, axis=0), 0.0)
        s *= 2
    return x


def _to_channel_major(src, dst):
    for j in range(SSD_WIDTH // LANES):
        sl = slice(j * LANES, (j + 1) * LANES)
        dst[sl, :] = src[:, sl].astype(F32).T


def _gate_norm_store(yt, zt, nw_ref, yz_ref):
    for g in range(SSD_GROUPS):
        sl = slice(g * SSD_GW, (g + 1) * SSD_GW)
        v = yt[sl, :] * _silu(zt[sl, :])
        ms = jnp.mean(v * v, axis=0, keepdims=True)
        yt[sl, :] = v * lax.rsqrt(ms + NORM_EPS)
    for j in range(SSD_WIDTH // LANES):
        sl = slice(j * LANES, (j + 1) * LANES)
        yz_ref[:, sl] = (yt[sl, :].T * nw_ref[:, sl]).astype(yz_ref.dtype)


def _group_bc(act, g):
    bg = act[:, SSD_WIDTH + g * SSD_STATE:SSD_WIDTH + (g + 1) * SSD_STATE].astype(BF16)
    cg = act[:, SSD_WIDTH + 1024 + g * SSD_STATE:SSD_WIDTH + 1024 + (g + 1) * SSD_STATE].astype(BF16)
    return bg, cg


SSD_CPS = 2


def _ssd_body(d_ref, z_ref, xs_ref, b_ref, c_ref, dt_ref, cw_ref, cb_ref, dtb_ref, alog_ref, nw_ref,
              yz_ref, ssmo_ref, cbuf, act, xt, zt, yt, state):
    q = SSD_CHUNK
    hp = SSD_HEAD_DIM

    @pl.when(pl.program_id(1) == 0)
    def _():
        state[...] = jnp.zeros_like(state)
        cbuf[q:, :] = jnp.zeros((q, SSD_CONV_DIM), BF16)

    t_i = lax.broadcasted_iota(jnp.int32, (q, 2 * q), 0)
    r_i = lax.broadcasted_iota(jnp.int32, (q, 2 * q), 1)
    shifts = [jnp.where(r_i == ((t_i - d) & (2 * q - 1)), 1.0, 0.0).astype(BF16) for d in range(SSD_CONV)]
    causal = lax.broadcasted_iota(jnp.int32, (q, q), 1) >= lax.broadcasted_iota(jnp.int32, (q, q), 0)

    def chunk(ci, carry):
        rows = pl.ds(pl.multiple_of(ci * q, q), q)
        cbuf[:q, 0:SSD_WIDTH] = xs_ref[rows, :]
        cbuf[:q, SSD_WIDTH:SSD_WIDTH + 1024] = b_ref[rows, :]
        cbuf[:q, SSD_WIDTH + 1024:] = c_ref[rows, :]
        for s0 in range(0, SSD_CONV_DIM, CONV_SEC):
            cs = slice(s0, s0 + CONV_SEC)
            both = cbuf[:, cs]
            acc = cb_ref[:, cs]
            for d in range(SSD_CONV):
                acc = acc + cw_ref[SSD_CONV - 1 - d:SSD_CONV - d, cs] * _dot(shifts[d], both)
            act[:, cs] = _silu(acc)
        cbuf[2 * q - CONV_KEEP:, :] = cbuf[q - CONV_KEEP:q, :]

        dt = jax.nn.softplus(dt_ref[rows, :] + dtb_ref[...])
        a_cs = _seg_cumsum_rows(dt * (-jnp.exp(alog_ref[...])), q) * LOG2E
        a_cst = a_cs.T
        dtt = dt.T
        _to_channel_major(act, xt)
        _to_channel_major(z_ref.at[rows, :], zt)

        for g in range(SSD_GROUPS):
            bg, cg = _group_bc(act, g)
            cbt = _dot_nt(bg, cg)
            stg = state[g * SSD_GW:(g + 1) * SSD_GW, :]
            yoff = _dot_nt(stg.astype(BF16), cg)
            for r in range(SSD_HPG):
                h = g * SSD_HPG + r
                hs = slice(h * hp, (h + 1) * hp)
                arow = a_cst[h:h + 1, :]
                acol = a_cs[:, h:h + 1]
                lt = jnp.where(causal, jnp.exp2(arow - acol), 0.0)
                wt = (cbt * lt).astype(BF16)
                xh = xt[hs, :]
                xdt = xh * dtt[h:h + 1, :]
                yt[hs, :] = (_dot(xdt.astype(BF16), wt) + yoff[r * hp:(r + 1) * hp, :] * jnp.exp2(arow)
                             + d_ref[h] * xh)
                atot = arow[:, q - 1:q]
                xdec = xdt * jnp.exp2(atot - arow)
                state[hs, :] = jnp.exp2(atot) * stg[r * hp:(r + 1) * hp, :] + _dot(xdec.astype(BF16), bg)

        _gate_norm_store(yt, zt, nw_ref, yz_ref.at[rows, :])
        return carry

    lax.fori_loop(0, SSD_CPS, chunk, 0)
    ssmo_ref[...] = state[...]


def _ssd_consts(conv_w, conv_b, dt_bias, a_log, ssd_norm_w):
    padl = lambda v: jnp.pad(v, (0, LANES - SSD_HEADS)).reshape(1, LANES)
    return (conv_w, conv_b.reshape(1, -1), padl(dt_bias), padl(a_log), ssd_norm_w.reshape(1, -1))


def _ssd_const_specs():
    const = lambda shape: pl.BlockSpec(shape, lambda *_: (0, 0))
    return [const((SSD_CONV, SSD_CONV_DIM)), const((1, SSD_CONV_DIM)), const((1, LANES)), const((1, LANES)),
            const((1, SSD_WIDTH))]


def _ssd_proj_specs(rows, rowf):
    return [pl.BlockSpec((rows, SSD_WIDTH), lambda *a: (rowf(*a), 0)),
            pl.BlockSpec((rows, SSD_WIDTH), lambda *a: (rowf(*a), 1)),
            pl.BlockSpec((rows, 1024), lambda *a: (rowf(*a), CB_B)),
            pl.BlockSpec((rows, 1024), lambda *a: (rowf(*a), CB_C)),
            pl.BlockSpec((rows, LANES), lambda *a: (rowf(*a), 0))]


def _ssd(proj, dt, nseq, nchunk, d_ssd, consts):
    m = proj.shape[0]
    nstep = nchunk // SSD_CPS
    rows = SSD_CPS * SSD_CHUNK
    rowf = lambda b, c: b * nstep + c
    chan = pltpu.VMEM((SSD_WIDTH, LANES), F32)
    return pl.pallas_call(
        _ssd_body,
        grid=(nseq, nstep),
        in_specs=[pl.BlockSpec(memory_space=pltpu.SMEM)] + _ssd_proj_specs(rows, rowf) + _ssd_const_specs(),
        out_specs=[pl.BlockSpec((rows, SSD_WIDTH), lambda b, c: (rowf(b, c), 0)),
                   pl.BlockSpec((None, SSD_WIDTH, SSD_STATE), lambda b, c: (b, 0, 0))],
        out_shape=[jax.ShapeDtypeStruct((m, SSD_WIDTH), BF16),
                   jax.ShapeDtypeStruct((nseq, SSD_WIDTH, SSD_STATE), F32)],
        scratch_shapes=[pltpu.VMEM((2 * SSD_CHUNK, SSD_CONV_DIM), BF16),
                        pltpu.VMEM((SSD_CHUNK, SSD_CONV_DIM), F32),
                        chan, chan, chan, pltpu.VMEM((SSD_WIDTH, SSD_STATE), F32)],
        compiler_params=_cparams(("arbitrary", "arbitrary")),
        name="ssd",
    )(d_ssd, proj, proj, proj, proj, dt, *consts)


SSD_PACK = 16
SSD_TOK = SSD_CHUNK // SSD_PACK
SSD_SUB = 4


def _ssd_decay_body(dt_ref, dtb_ref, alog_ref, o_ref):
    da = jax.nn.softplus(dt_ref[...] + dtb_ref[...]) * (-jnp.exp(alog_ref[...]))
    o_ref[...] = jnp.exp(jnp.sum(da.reshape(o_ref.shape[0], SSD_TOK, LANES), axis=1))


def _ssd_decay(dt, nseq, dtb, alog):
    const = lambda shape: pl.BlockSpec(shape, lambda i: (0, 0))
    return pl.pallas_call(
        _ssd_decay_body,
        grid=(1,),
        in_specs=[const((nseq * SSD_TOK, LANES)), const((1, LANES)), const((1, LANES))],
        out_specs=const((nseq, LANES)),
        out_shape=jax.ShapeDtypeStruct((nseq, LANES), F32),
        compiler_params=_cparams(("arbitrary",)),
        name="ssd_decay",
    )(dt, dtb, alog)


def _ssd_packed_body(dec_ref, d_ref, z_ref, xs_ref, b_ref, c_ref, dt_ref, cw_ref, cb_ref, dtb_ref, alog_ref,
                     nw_ref, convp_ref, ssm0_ref,
                     yz_ref, ssmo_ref, act, xt, zt, yt, xd, ear, bgs, cgs):
    q = SSD_CHUNK
    tl = SSD_TOK
    hp = SSD_HEAD_DIM
    tile = pl.program_id(0)
    sub = pl.program_id(1)

    @pl.when(sub == 0)
    def _shared():
        rowt = lax.broadcasted_iota(jnp.int32, (q, CONV_SEC), 0) & (tl - 1)
        for s0 in range(0, SSD_CONV_DIM, CONV_SEC):
            cs = slice(s0, s0 + CONV_SEC)
            if s0 < SSD_WIDTH:
                x = xs_ref[:, cs]
            elif s0 < SSD_WIDTH + 1024:
                x = b_ref[:, s0 - SSD_WIDTH:s0 - SSD_WIDTH + CONV_SEC]
            else:
                x = c_ref[:, s0 - SSD_WIDTH - 1024:s0 - SSD_WIDTH - 1024 + CONV_SEC]
            x = x.astype(F32)
            e = convp_ref[:, :, cs].reshape(q, CONV_SEC)
            acc = cb_ref[:, cs] + cw_ref[SSD_CONV - 1:SSD_CONV, cs] * x
            for d in range(1, SSD_CONV):
                prev = jnp.where(rowt >= d, pltpu.roll(x, d, axis=0), pltpu.roll(e, q - tl + d, axis=0))
                acc = acc + cw_ref[SSD_CONV - 1 - d:SSD_CONV - d, cs] * prev
            act[:, cs] = _silu(acc)

        dt = jax.nn.softplus(dt_ref[...] + dtb_ref[...])
        da = dt * (-jnp.exp(alog_ref[...]))
        a_cs = _seg_cumsum_rows(da, tl)
        a_tot = a_cs + _seg_revsum_rows(da, tl) - da
        a_cst, a_tott, dtt = a_cs.T, a_tot.T, dt.T
        ea = jnp.exp(a_cst)
        _to_channel_major(act, xt)
        _to_channel_major(z_ref, zt)

        s_i = lax.broadcasted_iota(jnp.int32, (q, q), 0)
        l_i = lax.broadcasted_iota(jnp.int32, (q, q), 1)
        causal = (l_i >= s_i) & (jnp.right_shift(l_i, 3) == jnp.right_shift(s_i, 3))
        for g in range(SSD_GROUPS):
            bg, cg = _group_bc(act, g)
            bgs[g] = bg
            cgs[g] = cg
            cbt = _dot_nt(bg, cg)
            for r in range(SSD_HPG):
                h = g * SSD_HPG + r
                hs = slice(h * hp, (h + 1) * hp)
                arow = a_cst[h:h + 1, :]
                acol = a_cs[:, h:h + 1]
                lt = jnp.where(causal, jnp.exp(arow - acol), 0.0)
                wt = (cbt * lt).astype(BF16)
                xh = xt[hs, :]
                xdt = xh * dtt[h:h + 1, :]
                yt[hs, :] = _dot(xdt.astype(BF16), wt) + d_ref[h] * xh
                xd[hs, :] = xdt * jnp.exp(a_tott[h:h + 1, :] - arow)
                ear[hs, :] = jnp.broadcast_to(ea[h:h + 1, :], (hp, q))

    lane_seq = jnp.right_shift(lax.broadcasted_iota(jnp.int32, (SSD_GW, q), 1), 3)

    def seq_body(sl, carry):
        sq = sub * SSD_SUB + sl
        inseq = lane_seq == sq
        dbase = (tile * SSD_PACK + sq) * SSD_HEADS
        for g in range(SSD_GROUPS):
            rows = slice(g * SSD_GW, (g + 1) * SSD_GW)
            stg = ssm0_ref[sl, rows, :]
            yo = _dot_nt(stg.astype(BF16), cgs[g])
            yt[rows, :] = yt[rows, :] + jnp.where(inseq, yo * ear[rows, :], 0.0)
            upd = _dot(jnp.where(inseq, xd[rows, :], 0.0).astype(BF16), bgs[g])
            for r in range(SSD_HPG):
                h = g * SSD_HPG + r
                ssmo_ref[sl, h * hp:(h + 1) * hp, :] = (dec_ref[dbase + h] * stg[r * hp:(r + 1) * hp, :]
                                                        + upd[r * hp:(r + 1) * hp, :])
        return carry

    lax.fori_loop(0, SSD_SUB, seq_body, 0)

    @pl.when(sub == SSD_PACK // SSD_SUB - 1)
    def _():
        _gate_norm_store(yt, zt, nw_ref, yz_ref)


def _ssd_packed(proj, dt, nseq, d_ssd, consts, conv0, ssm0):
    m = proj.shape[0]
    ntile = nseq // SSD_PACK
    nsub = SSD_PACK // SSD_SUB
    dec = _ssd_decay(dt, nseq, consts[2], consts[3])[:, :SSD_HEADS].reshape(-1)
    convp = jnp.pad(conv0, ((0, 0), (SSD_TOK - (SSD_CONV - 1), 0), (0, 0)))
    chan = pltpu.VMEM((SSD_WIDTH, LANES), F32)
    st_spec = pl.BlockSpec((SSD_SUB, SSD_WIDTH, SSD_STATE), lambda i, s: (i * nsub + s, 0, 0))
    return pl.pallas_call(
        _ssd_packed_body,
        grid=(ntile, nsub),
        in_specs=[pl.BlockSpec(memory_space=pltpu.SMEM), pl.BlockSpec(memory_space=pltpu.SMEM)]
        + _ssd_proj_specs(SSD_CHUNK, lambda i, s: i) + _ssd_const_specs()
        + [pl.BlockSpec((SSD_PACK, SSD_TOK, SSD_CONV_DIM), lambda i, s: (i, 0, 0)), st_spec],
        out_specs=[pl.BlockSpec((SSD_CHUNK, SSD_WIDTH), lambda i, s: (i, 0)), st_spec],
        out_shape=[jax.ShapeDtypeStruct((m, SSD_WIDTH), BF16),
                   jax.ShapeDtypeStruct((nseq, SSD_WIDTH, SSD_STATE), F32)],
        scratch_shapes=[pltpu.VMEM((SSD_CHUNK, SSD_CONV_DIM), F32), chan, chan, chan, chan, chan,
                        pltpu.VMEM((SSD_GROUPS, SSD_CHUNK, SSD_STATE), BF16),
                        pltpu.VMEM((SSD_GROUPS, SSD_CHUNK, SSD_STATE), BF16)],
        compiler_params=_cparams(("arbitrary", "arbitrary")),
        name="ssd_packed",
    )(dec, d_ssd, proj, proj, proj, proj, dt, *consts, convp, ssm0)


S5_GB = 8
S5_SCAN_SEQS = 2


def _s5_body(*refs, q, scan):
    if scan:
        u_ref, toep_ref, bin_ref, coff_ref, aqr_ref, aqi_ref, y_ref, so_ref, sr_s, si_s = refs
        s0_ref = None
    else:
        u_ref, toep_ref, bin_ref, coff_ref, aqr_ref, aqi_ref, s0_ref, y_ref, so_ref = refs
    k = S5_GROUP_SIZE
    p = S5_STATE
    gp = S5_GB * p
    w = u_ref.shape[2]
    nsq = w // LANES

    def chunk_in(gl):
        return u_ref[:, gl * k:(gl + 1) * k, :].reshape(q * k, w)

    if scan:
        for gl in range(S5_GB):
            bo = _dot(bin_ref[gl], chunk_in(gl).astype(BF16))
            sr_s[gl * p:(gl + 1) * p, :] = bo[:p, :]
            si_s[gl * p:(gl + 1) * p, :] = bo[p:, :]
        lane = lax.broadcasted_iota(jnp.int32, (gp, w), 1) & (LANES - 1)
        xr, xi = sr_s[...], si_s[...]
        pr = jnp.concatenate([aqr_ref[...].reshape(gp, LANES)] * nsq, axis=1)
        pi = jnp.concatenate([aqi_ref[...].reshape(gp, LANES)] * nsq, axis=1)
        s = 1
        while s < LANES:
            m = lane >= s
            tr = jnp.where(m, pltpu.roll(xr, s, axis=1), 0.0)
            ti = jnp.where(m, pltpu.roll(xi, s, axis=1), 0.0)
            xr, xi = xr + (pr * tr - pi * ti), xi + (pr * ti + pi * tr)
            if 2 * s < LANES:
                pr, pi = pr * pr - pi * pi, 2.0 * pr * pi
            s *= 2
        for sq in range(nsq):
            last = slice((sq + 1) * LANES - 1, (sq + 1) * LANES)
            so_ref[sq, :, :p, :] = xr[:, last].reshape(S5_GB, p, 1)
            so_ref[sq, :, p:, :] = xi[:, last].reshape(S5_GB, p, 1)
        m1 = lane >= 1
        sr_s[...] = jnp.where(m1, pltpu.roll(xr, 1, axis=1), 0.0)
        si_s[...] = jnp.where(m1, pltpu.roll(xi, 1, axis=1), 0.0)

    for gl in range(S5_GB):
        z = chunk_in(gl)
        if scan:
            sin_r, sin_i = sr_s[gl * p:(gl + 1) * p, :], si_s[gl * p:(gl + 1) * p, :]
        else:
            bo = _dot_hi(bin_ref[gl], z)
            aqr, aqi = aqr_ref[gl], aqi_ref[gl]
            sin_r, sin_i = s0_ref[gl, :p, :], s0_ref[gl, p:, :]
            so_ref[gl, :p, :] = aqr * sin_r - aqi * sin_i + bo[:p, :]
            so_ref[gl, p:, :] = aqr * sin_i + aqi * sin_r + bo[p:, :]
        s_in = jnp.concatenate([sin_r, sin_i], axis=0).astype(BF16)
        y = _dot(toep_ref[gl], z.astype(BF16)) + _dot(coff_ref[gl], s_in)
        y_ref[:, gl * k:(gl + 1) * k, :] = y.reshape(q, k, w).astype(y_ref.dtype)


def _s5(ut, q, scan, prep, s0t=None):
    ctot = ut.shape[2]
    lw = min(S5_SCAN_SEQS * LANES, ctot) if scan else LANES
    nlb = ctot // lw
    ngb = S5_GROUPS // S5_GB
    qk = q * S5_GROUP_SIZE
    gspec = lambda r, c, cb=0: pl.BlockSpec((S5_GB, r, c), lambda gb, lb: (gb, 0, cb))
    in_specs = [pl.BlockSpec((q, S5_GB * S5_GROUP_SIZE, lw), lambda gb, lb: (0, gb, lb)),
                gspec(qk, qk), gspec(2 * S5_STATE, qk, (S5_Q - q) * S5_GROUP_SIZE // qk),
                gspec(qk, 2 * S5_STATE), gspec(S5_STATE, LANES), gspec(S5_STATE, LANES)]
    args = [ut, prep["toep"], prep["binb"] if scan else prep["bin"], prep["coff"], *prep["aq"][q]]
    scratch = []
    if scan:
        so_shape = jax.ShapeDtypeStruct((ctot // LANES, S5_GROUPS, 2 * S5_STATE, 1), F32)
        so_spec = pl.BlockSpec((lw // LANES, S5_GB, 2 * S5_STATE, 1), lambda gb, lb: (lb, gb, 0, 0))
        scratch = [pltpu.VMEM((S5_GB * S5_STATE, lw), F32)] * 2
    else:
        in_specs.append(gspec(2 * S5_STATE, LANES))
        args.append(s0t)
        so_shape = jax.ShapeDtypeStruct((S5_GROUPS, 2 * S5_STATE, LANES), F32)
        so_spec = gspec(2 * S5_STATE, LANES)
    return pl.pallas_call(
        functools.partial(_s5_body, q=q, scan=scan),
        grid=(ngb, nlb),
        in_specs=in_specs,
        out_specs=[pl.BlockSpec((q, S5_GB * S5_GROUP_SIZE, lw), lambda gb, lb: (0, gb, lb)), so_spec],
        out_shape=[jax.ShapeDtypeStruct(ut.shape, BF16), so_shape],
        scratch_shapes=scratch,
        compiler_params=_cparams(("arbitrary", "arbitrary")),
        name="s5_scan" if scan else "s5_step",
    )(*args)


OUT_ROWS = 512


def _out_body(y5_ref, u5_ref, z5_ref, ga_ref, gb_ref, yz_ref, x_ref, d5_ref, wglu_ref, bglu_ref,
              wd5_ref, wda_ref, wout_ref, fnw_ref, o_ref):
    f32 = lambda r: r[...].astype(F32)
    y5 = jax.nn.gelu(f32(y5_ref) + d5_ref[...] * f32(u5_ref))
    y5 = y5 * _sigmoid(_dot(y5.astype(BF16), wglu_ref[...]) + bglu_ref[...])
    y5 = y5 * _silu(f32(z5_ref))
    y_b = _dot(y5.astype(BF16), wd5_ref[...])
    y_a = _dot(yz_ref[...], wda_ref[...])
    mix = _sigmoid(f32(ga_ref)) * y_a + _sigmoid(f32(gb_ref)) * y_b
    out = x_ref[...] + _dot(mix.astype(BF16), wout_ref[...])
    ms = jnp.mean(out * out, axis=-1, keepdims=True)
    o_ref[...] = out * lax.rsqrt(ms + NORM_EPS) * fnw_ref[...]


def _out(y5, u5, proj, yz, x2d, prm):
    m = x2d.shape[0]
    rowb = lambda w, cb: pl.BlockSpec((OUT_ROWS, w), lambda i: (i, cb))
    const = lambda r, c: pl.BlockSpec((r, c), lambda i: (0, 0))
    return pl.pallas_call(
        _out_body,
        grid=(m // OUT_ROWS,),
        in_specs=[rowb(1024, 0), rowb(1024, 0), rowb(1024, CB_Z5), rowb(1024, CB_GA), rowb(1024, CB_GB),
                  rowb(SSD_WIDTH, 0), rowb(1024, 0),
                  const(1, 1024), const(1024, 1024), const(1, 1024), const(1024, 1024),
                  const(SSD_WIDTH, 1024), const(1024, 1024), const(1, 1024)],
        out_specs=rowb(1024, 0),
        out_shape=jax.ShapeDtypeStruct((m, D_MODEL), F32),
        compiler_params=_cparams(("arbitrary",)),
        name="out_stage",
    )(y5, u5, proj, proj, proj, yz, x2d, prm["d_s5"], prm["w_glu"], prm["b_glu"],
      prm["w_down_s5"], prm["w_down_ssd"], prm["w_out"], prm["final_norm_w"])


def _run(x, conv0, ssm0, re0, im0, prm):
    nseq, seqlen, _ = x.shape
    m = nseq * seqlen
    prompt = conv0 is None
    x2d = x.reshape(m, D_MODEL)
    proj, dt, u5 = _in_proj(x2d, prm["norm_w"], prm["w_cat"], prm["w_dt"])
    conv_new = proj.reshape(nseq, seqlen, PROJ_COLS)[:, seqlen - (SSD_CONV - 1):,
                                                     SSD_WIDTH:SSD_WIDTH + SSD_CONV_DIM].astype(F32)

    if prompt:
        yz, ssm_new = _ssd(proj, dt, nseq, seqlen // SSD_CHUNK, prm["d_ssd"], prm["ssd_consts"])
    else:
        yz, ssm_new = _ssd_packed(proj, dt, nseq, prm["d_ssd"], prm["ssd_consts"], conv0,
                                  ssm0.reshape(nseq, SSD_WIDTH, SSD_STATE))

    q = S5_Q if prompt else seqlen
    ut = jnp.transpose(u5.reshape(m // q, q, S5_WIDTH), (1, 2, 0))
    if prompt:
        yt, so = _s5(ut, q, True, prm["s5"])
        so = so.reshape(nseq, S5_GROUPS, 2, S5_STATE)
        re_new, im_new = so[:, :, 0, :], so[:, :, 1, :]
    else:
        s0t = jnp.transpose(jnp.concatenate([re0, im0], axis=-1), (1, 2, 0))
        yt, so = _s5(ut, q, False, prm["s5"], s0t)
        so = jnp.transpose(so, (2, 0, 1))
        re_new, im_new = so[:, :, :S5_STATE], so[:, :, S5_STATE:]
    y5 = jnp.transpose(yt, (2, 0, 1)).reshape(m, S5_WIDTH)

    out = _out(y5, u5, proj, yz, x2d, prm)
    return (out.reshape(nseq, seqlen, D_MODEL), conv_new[None],
            ssm_new.reshape(1, nseq, SSD_HEADS, SSD_HEAD_DIM, SSD_STATE), re_new[None], im_new[None])


def kernel(x_prompt, x_sample, state_conv, state_ssm, state_s5_re, state_s5_im, norm_w, w_in, conv_w, conv_b,
           dt_bias, A_log, D_ssd, ssd_norm_w, w_down_ssd, lam_re, lam_im, log_dt, B_re, B_im, C_re, C_im, D_s5,
           w_glu, b_glu, w_down_s5, w_out, final_norm_w):
    assert norm_w.shape[0] == 1, "single-layer trunk"
    assert x_prompt.shape[1] % SSD_CHUNK == 0 and x_prompt.shape[1] // S5_Q == LANES
    assert x_sample.shape[1] == SSD_TOK and x_sample.shape[0] == LANES
    w = w_in[0]
    o_dt = SSD_WIDTH + SSD_CONV_DIM
    w_cat = jnp.concatenate([w[:, :o_dt], w[:, o_dt + SSD_HEADS:]], axis=1).astype(BF16)
    w_dt = jnp.pad(w[:, o_dt:o_dt + SSD_HEADS], ((0, 0), (0, LANES - SSD_HEADS))).astype(BF16)
    prm = dict(norm_w=norm_w[0].reshape(1, -1), w_cat=w_cat, w_dt=w_dt,
               ssd_consts=_ssd_consts(conv_w[0], conv_b[0], dt_bias[0], A_log[0], ssd_norm_w[0]), d_ssd=D_ssd[0],
               s5=_s5_prep(lam_re[0], lam_im[0], log_dt[0], B_re[0], B_im[0], C_re[0], C_im[0]),
               d_s5=D_s5[0].reshape(1, -1), w_glu=w_glu[0].astype(BF16), b_glu=b_glu[0].reshape(1, -1),
               w_down_s5=w_down_s5[0].astype(BF16), w_down_ssd=w_down_ssd[0].astype(BF16),
               w_out=w_out[0].astype(BF16), final_norm_w=final_norm_w.reshape(1, -1))

    y_p, conv_p, ssm_p, re_p, im_p = _run(x_prompt, None, None, None, None, prm)
    y_s, conv_s, ssm_s, re_s, im_s = _run(x_sample, state_conv[0], state_ssm[0], state_s5_re[0],
                                          state_s5_im[0], prm)
    return (y_p, y_s, conv_p, ssm_p, re_p, im_p, conv_s, ssm_s, re_s, im_s)
```

```python
import functools

import jax
import jax.numpy as jnp
from jax import lax
from jax.experimental import pallas as pl
from jax.experimental.pallas import tpu as pltpu

F32 = jnp.float32
BF16 = jnp.bfloat16

D_MODEL = 1024
SSD_WIDTH = 2048
SSD_HEAD_DIM = 64
SSD_HEADS = 32
SSD_GROUPS = 8
SSD_HPG = SSD_HEADS // SSD_GROUPS
SSD_GW = SSD_WIDTH // SSD_GROUPS
SSD_STATE = 128
SSD_CONV = 4
SSD_CHUNK = 128
SSD_CONV_DIM = 4096
S5_WIDTH = 1024
S5_GROUP_SIZE = 16
S5_GROUPS = 64
S5_STATE = 64
NORM_EPS = 1e-6
LOG2E = 1.4426950408889634

LANES = 128
SUBLANES = 8

PROJ_COLS = 10240
COL_TILE = 1024
ROW_TILE = 2048
CB_B, CB_C, CB_U5, CB_Z5, CB_GA, CB_GB = 4, 5, 6, 7, 8, 9

VMEM_LIMIT = 56 * 1024 * 1024


def _cparams(sem, flags=None):
    return pltpu.CompilerParams(dimension_semantics=sem, vmem_limit_bytes=VMEM_LIMIT, flags=flags)


def _sigmoid(x):
    return 0.5 * jnp.tanh(0.5 * x) + 0.5


def _silu(x):
    h = 0.5 * x
    return h * jnp.tanh(h) + h


def _dot(a, b):
    return jnp.dot(a, b, preferred_element_type=F32)


def _dot_nt(a, b):
    return lax.dot_general(a, b, (((1,), (1,)), ((), ())), preferred_element_type=F32)


def _dot_hi(a, b):
    a_hi = a.astype(BF16)
    a_lo = (a - a_hi.astype(F32)).astype(BF16)
    if b.dtype == BF16:
        return _dot(a_hi, b) + _dot(a_lo, b)
    b_hi = b.astype(BF16)
    b_lo = (b - b_hi.astype(F32)).astype(BF16)
    return _dot(a_hi, b_hi) + _dot(a_hi, b_lo) + _dot(a_lo, b_hi)


S5_Q = 16
PREP_GB = 8


def _zoh(lr, li, ldt):
    step = jnp.exp(ldt)
    mag = jnp.exp(lr * step)
    return mag * jnp.cos(li * step), mag * jnp.sin(li * step)


def _s5_prep_body(rowp_ref, cre_ref, cim_ref, b1_ref, b2_ref,
                  toep_ref, coff_ref, bin_ref, binb_ref, aq16r_ref, aq16i_ref, aq8r_ref, aq8i_ref, ca_s):
    k, p, q = S5_GROUP_SIZE, S5_STATE, S5_Q
    qk = q * k
    lr, li = rowp_ref[:, 0:1, :], rowp_ref[:, 1:2, :]
    ar_l, ai_l = _zoh(lr, li, rowp_ref[:, 2:3, :])
    den = lr * lr + li * li
    nr = ar_l - 1.0
    cr_l = (nr * lr + ai_l * li) / den
    ci_l = (ai_l * lr - nr * li) / den

    def on_sublanes(v):
        return jnp.stack([jnp.broadcast_to(v[g], (2 * p, 2 * p)).T for g in range(PREP_GB)])

    wide = lambda v: jnp.concatenate([v] * (qk // (2 * p)), axis=-1)

    ar, ai, cr, ci = on_sublanes(ar_l), on_sublanes(ai_l), on_sublanes(cr_l), on_sublanes(ci_l)
    sgn_row = jnp.where(lax.broadcasted_iota(jnp.int32, (PREP_GB, 2 * p, qk), 1) < p, -1.0, 1.0)
    b1, b2 = b1_ref[...], b2_ref[...]
    crw, ciw = wide(cr), sgn_row * wide(ci)
    bb = crw * b1 + ciw * b2
    bb_sw = crw * b2 - ciw * b1
    lane_t = jnp.right_shift(lax.broadcasted_iota(jnp.int32, (PREP_GB, 2 * p, qk), 2), 4)
    pr, pi = jnp.ones_like(ar), jnp.zeros_like(ar)
    pwr = jnp.zeros((PREP_GB, 2 * p, qk), F32)
    pwi = jnp.zeros((PREP_GB, 2 * p, qk), F32)
    for j in range(q + 1):
        if j < q:
            sel = lane_t == (q - 1 - j)
            pwr = jnp.where(sel, wide(pr), pwr)
            pwi = jnp.where(sel, wide(pi), pwi)
        if j == q // 2:
            aq8r_ref[...] = pr[:, :p, :]
            aq8i_ref[...] = pi[:, :p, :]
        if j == q:
            aq16r_ref[...] = pr[:, :p, :]
            aq16i_ref[...] = pi[:, :p, :]
        pr, pi = pr * ar - pi * ai, pr * ai + pi * ar
    binv = pwr * bb + sgn_row * pwi * bb_sw
    bin_ref[...] = binv
    binb_ref[...] = binv.astype(BF16)

    ar, ai = ar_l, ai_l
    low = lax.broadcasted_iota(jnp.int32, (PREP_GB, 1, 2 * p), 2) < p
    sgn_lane = jnp.where(low, 1.0, -1.0)
    cre, cim = cre_ref[...], cim_ref[...]
    pr, pi = jnp.ones_like(ar), jnp.zeros_like(ar)
    for j in range(q + 1):
        p1 = jnp.where(low, pr, pi)
        p2 = jnp.where(low, pi, pr)
        ca_s[:, j * k:(j + 1) * k, :] = sgn_lane * cre * p1 - cim * p2
        pr, pi = pr * ar - pi * ai, pr * ai + pi * ar
    coff_ref[...] = ca_s[:, k:, :].astype(BF16)

    lane_blk = jnp.right_shift(lax.broadcasted_iota(jnp.int32, (qk, qk), 1), 4)
    for g in range(PREP_GB):
        km = _dot_hi(ca_s[g, :qk, :], bb[g])
        tp = jnp.where(lane_blk == 0, km, 0.0)
        for t in range(1, q):
            sh = jnp.concatenate([jnp.zeros((t * k, qk), F32), km[:qk - t * k, :]], axis=0)
            tp = jnp.where(lane_blk == t, sh, tp)
        toep_ref[g] = tp.astype(BF16)


def _s5_prep(lam_re, lam_im, log_dt, b_re, b_im, c_re, c_im):
    g, p, k, q = S5_GROUPS, S5_STATE, S5_GROUP_SIZE, S5_Q
    qk = q * k
    dup = lambda a: jnp.concatenate([a, a], axis=-1)
    ldt = jnp.broadcast_to(log_dt[:, None], (g, 2 * p))
    rowp = jnp.stack([dup(lam_re), dup(lam_im), ldt], axis=1)
    b1 = jnp.tile(jnp.concatenate([b_re, b_im], axis=1), (1, 1, q))
    b2 = jnp.tile(jnp.concatenate([b_im, b_re], axis=1), (1, 1, q))
    blk = lambda r, c: pl.BlockSpec((PREP_GB, r, c), lambda i: (i, 0, 0))
    shp = lambda r, c, dt: jax.ShapeDtypeStruct((g, r, c), dt)
    toep, coff, bin_, binb, aq16r, aq16i, aq8r, aq8i = pl.pallas_call(
        _s5_prep_body,
        grid=(g // PREP_GB,),
        in_specs=[blk(3, 2 * p), blk(k, 2 * p), blk(k, 2 * p), blk(2 * p, qk), blk(2 * p, qk)],
        out_specs=[blk(qk, qk), blk(qk, 2 * p), blk(2 * p, qk), blk(2 * p, qk),
                   blk(p, LANES), blk(p, LANES), blk(p, LANES), blk(p, LANES)],
        out_shape=[shp(qk, qk, BF16), shp(qk, 2 * p, BF16), shp(2 * p, qk, F32), shp(2 * p, qk, BF16),
                   shp(p, LANES, F32), shp(p, LANES, F32), shp(p, LANES, F32), shp(p, LANES, F32)],
        scratch_shapes=[pltpu.VMEM((PREP_GB, (q + 1) * k, 2 * p), F32)],
        compiler_params=_cparams(("arbitrary",)),
        name="s5_prep",
    )(rowp, dup(c_re), dup(c_im), b1, b2)
    return dict(toep=toep, coff=coff, bin=bin_, binb=binb,
                aq={16: (aq16r, aq16i), 8: (aq8r, aq8i)})


def _in_proj_body(x_ref, nw_ref, w_ref, wdt_ref, o_ref, dt_ref, u5_ref, h_ref):
    j = pl.program_id(1)

    @pl.when(j == 0)
    def _():
        x = x_ref[...]
        ms = jnp.mean(x * x, axis=-1, keepdims=True)
        h = (x * lax.rsqrt(ms + NORM_EPS) * nw_ref[...]).astype(BF16)
        h_ref[...] = h
        dt_ref[...] = _dot(h, wdt_ref[...])

    o = _dot(h_ref[...], w_ref[...]).astype(BF16)
    o_ref[...] = o

    @pl.when(j == CB_U5)
    def _():
        u5_ref[...] = o


def _in_proj(x2d, norm_w, w_cat, w_dt):
    m = x2d.shape[0]
    rt = min(ROW_TILE, m)
    return pl.pallas_call(
        _in_proj_body,
        grid=(m // rt, PROJ_COLS // COL_TILE),
        in_specs=[pl.BlockSpec((rt, D_MODEL), lambda i, j: (i, 0)),
                  pl.BlockSpec((1, D_MODEL), lambda i, j: (0, 0)),
                  pl.BlockSpec((D_MODEL, COL_TILE), lambda i, j: (0, j)),
                  pl.BlockSpec((D_MODEL, LANES), lambda i, j: (0, 0))],
        out_specs=[pl.BlockSpec((rt, COL_TILE), lambda i, j: (i, j)),
                   pl.BlockSpec((rt, LANES), lambda i, j: (i, 0)),
                   pl.BlockSpec((rt, COL_TILE), lambda i, j: (i, 0))],
        out_shape=[jax.ShapeDtypeStruct((m, PROJ_COLS), BF16), jax.ShapeDtypeStruct((m, LANES), F32),
                   jax.ShapeDtypeStruct((m, S5_WIDTH), BF16)],
        scratch_shapes=[pltpu.VMEM((rt, D_MODEL), BF16)],
        compiler_params=_cparams(("arbitrary", "arbitrary")),
        name="in_proj",
    )(x2d, norm_w, w_cat, w_dt)


CONV_PAD = SUBLANES
CONV_SEC = 512


def _seg_cumsum_rows(x, seg):
    t = lax.broadcasted_iota(jnp.int32, x.shape, 0) & (seg - 1)
    s = 1
    while s < seg:
        x = x + jnp.where(t >= s, pltpu.roll(x, s, axis=0), 0.0)
        s *= 2
    return x


def _seg_revsum_rows(x, seg):
    n = x.shape[0]
    t = lax.broadcasted_iota(jnp.int32, x.shape, 0) & (seg - 1)
    s = 1
    while s < seg:
        x = x + jnp.where(t + s < seg, pltpu.roll(x, n - s, axis=0), 0.0)
        s *= 2
    return x


def _to_channel_major(src, dst):
    for j in range(SSD_WIDTH // LANES):
        sl = slice(j * LANES, (j + 1) * LANES)
        dst[sl, :] = src[:, sl].astype(F32).T


def _store_time_major(yt, y_ref):
    for j in range(SSD_WIDTH // LANES):
        sl = slice(j * LANES, (j + 1) * LANES)
        y_ref[:, sl] = yt[sl, :].T.astype(y_ref.dtype)


def _group_bc(act, g):
    bg = act[:, SSD_WIDTH + g * SSD_STATE:SSD_WIDTH + (g + 1) * SSD_STATE].astype(BF16)
    cg = act[:, SSD_WIDTH + 1024 + g * SSD_STATE:SSD_WIDTH + 1024 + (g + 1) * SSD_STATE].astype(BF16)
    return bg, cg


SSD_CPS = 2


def _ssd_body(d_ref, xs_ref, b_ref, c_ref, dt_ref, cw_ref, cb_ref, dtb_ref, alog_ref,
              y_ref, ssmo_ref, cbuf, act, xt, yt, state):
    q = SSD_CHUNK
    hp = SSD_HEAD_DIM

    k1 = SSD_CONV - 1
    base = CONV_PAD

    @pl.when(pl.program_id(1) == 0)
    def _():
        state[...] = jnp.zeros_like(state)
        cbuf[base - k1:base, :] = jnp.zeros((k1, SSD_CONV_DIM), F32)

    causal = lax.broadcasted_iota(jnp.int32, (q, q), 1) >= lax.broadcasted_iota(jnp.int32, (q, q), 0)

    def chunk(ci, carry):
        rows = pl.ds(pl.multiple_of(ci * q, q), q)
        cbuf[base:, 0:SSD_WIDTH] = xs_ref[rows, :].astype(F32)
        cbuf[base:, SSD_WIDTH:SSD_WIDTH + 1024] = b_ref[rows, :].astype(F32)
        cbuf[base:, SSD_WIDTH + 1024:] = c_ref[rows, :].astype(F32)
        for s0 in range(0, SSD_CONV_DIM, CONV_SEC):
            cs = slice(s0, s0 + CONV_SEC)
            acc = cb_ref[:, cs]
            for j in range(SSD_CONV):
                acc = acc + cw_ref[j:j + 1, cs] * cbuf[base - k1 + j:base - k1 + j + q, cs]
            act[:, cs] = _silu(acc)
        cbuf[base - k1:base, :] = cbuf[base + q - k1:base + q, :]

        dt = jax.nn.softplus(dt_ref[rows, :] + dtb_ref[...])
        a_cs = _seg_cumsum_rows(dt * (-jnp.exp(alog_ref[...])), q) * LOG2E
        a_cst = a_cs.T
        dtt = dt.T
        _to_channel_major(act, xt)

        for g in range(SSD_GROUPS):
            bg, cg = _group_bc(act, g)
            cbt = _dot_nt(bg, cg)
            stg = state[g * SSD_GW:(g + 1) * SSD_GW, :]
            yoff = _dot_nt(stg.astype(BF16), cg)
            for r in range(SSD_HPG):
                h = g * SSD_HPG + r
                hs = slice(h * hp, (h + 1) * hp)
                arow = a_cst[h:h + 1, :]
                acol = a_cs[:, h:h + 1]
                lt = jnp.where(causal, jnp.exp2(arow - acol), 0.0)
                wt = (cbt * lt).astype(BF16)
                xh = xt[hs, :]
                xdt = xh * dtt[h:h + 1, :]
                yt[hs, :] = (_dot(xdt.astype(BF16), wt) + yoff[r * hp:(r + 1) * hp, :] * jnp.exp2(arow)
                             + d_ref[h] * xh)
                atot = arow[:, q - 1:q]
                xdec = xdt * jnp.exp2(atot - arow)
                state[hs, :] = jnp.exp2(atot) * stg[r * hp:(r + 1) * hp, :] + _dot(xdec.astype(BF16), bg)

        _store_time_major(yt, y_ref.at[rows, :])
        return carry

    lax.fori_loop(0, SSD_CPS, chunk, 0)
    ssmo_ref[...] = state[...]


def _ssd_consts(conv_w, conv_b, dt_bias, a_log):
    padl = lambda v: jnp.pad(v, (0, LANES - SSD_HEADS)).reshape(1, LANES)
    return (conv_w, conv_b.reshape(1, -1), padl(dt_bias), padl(a_log))


def _ssd_const_specs():
    const = lambda shape: pl.BlockSpec(shape, lambda *_: (0, 0))
    return [const((SSD_CONV, SSD_CONV_DIM)), const((1, SSD_CONV_DIM)), const((1, LANES)), const((1, LANES))]


def _ssd_proj_specs(rows, rowf):
    return [pl.BlockSpec((rows, SSD_WIDTH), lambda *a: (rowf(*a), 1)),
            pl.BlockSpec((rows, 1024), lambda *a: (rowf(*a), CB_B)),
            pl.BlockSpec((rows, 1024), lambda *a: (rowf(*a), CB_C)),
            pl.BlockSpec((rows, LANES), lambda *a: (rowf(*a), 0))]


def _ssd(proj, dt, nseq, nchunk, d_ssd, consts):
    m = proj.shape[0]
    nstep = nchunk // SSD_CPS
    rows = SSD_CPS * SSD_CHUNK
    rowf = lambda b, c: b * nstep + c
    chan = pltpu.VMEM((SSD_WIDTH, LANES), F32)
    return pl.pallas_call(
        _ssd_body,
        grid=(nseq, nstep),
        in_specs=[pl.BlockSpec(memory_space=pltpu.SMEM)] + _ssd_proj_specs(rows, rowf) + _ssd_const_specs(),
        out_specs=[pl.BlockSpec((rows, SSD_WIDTH), lambda b, c: (rowf(b, c), 0)),
                   pl.BlockSpec((None, SSD_WIDTH, SSD_STATE), lambda b, c: (b, 0, 0))],
        out_shape=[jax.ShapeDtypeStruct((m, SSD_WIDTH), BF16),
                   jax.ShapeDtypeStruct((nseq, SSD_WIDTH, SSD_STATE), F32)],
        scratch_shapes=[pltpu.VMEM((CONV_PAD + SSD_CHUNK, SSD_CONV_DIM), F32),
                        pltpu.VMEM((SSD_CHUNK, SSD_CONV_DIM), F32),
                        chan, chan, pltpu.VMEM((SSD_WIDTH, SSD_STATE), F32)],
        compiler_params=_cparams(("arbitrary", "arbitrary")),
        name="ssd",
    )(d_ssd, proj, proj, proj, dt, *consts)


SSD_PACK = 16
SSD_TOK = SSD_CHUNK // SSD_PACK
SSD_SUB = 4


def _ssd_decay_body(dt_ref, dtb_ref, alog_ref, o_ref):
    da = jax.nn.softplus(dt_ref[...] + dtb_ref[...]) * (-jnp.exp(alog_ref[...]))
    o_ref[...] = jnp.exp(jnp.sum(da.reshape(o_ref.shape[0], SSD_TOK, LANES), axis=1))


def _ssd_decay(dt, nseq, dtb, alog):
    const = lambda shape: pl.BlockSpec(shape, lambda i: (0, 0))
    return pl.pallas_call(
        _ssd_decay_body,
        grid=(1,),
        in_specs=[const((nseq * SSD_TOK, LANES)), const((1, LANES)), const((1, LANES))],
        out_specs=const((nseq, LANES)),
        out_shape=jax.ShapeDtypeStruct((nseq, LANES), F32),
        compiler_params=_cparams(("arbitrary",)),
        name="ssd_decay",
    )(dt, dtb, alog)


def _ssd_packed_body(dec_ref, d_ref, xs_ref, b_ref, c_ref, dt_ref, cw_ref, cb_ref, dtb_ref, alog_ref,
                     convp_ref, ssm0_ref,
                     y_ref, ssmo_ref, act, xt, yt, xd, ear, bgs, cgs):
    q = SSD_CHUNK
    tl = SSD_TOK
    hp = SSD_HEAD_DIM
    tile = pl.program_id(0)
    sub = pl.program_id(1)

    @pl.when(sub == 0)
    def _shared():
        rowt = lax.broadcasted_iota(jnp.int32, (q, CONV_SEC), 0) & (tl - 1)
        for s0 in range(0, SSD_CONV_DIM, CONV_SEC):
            cs = slice(s0, s0 + CONV_SEC)
            if s0 < SSD_WIDTH:
                x = xs_ref[:, cs]
            elif s0 < SSD_WIDTH + 1024:
                x = b_ref[:, s0 - SSD_WIDTH:s0 - SSD_WIDTH + CONV_SEC]
            else:
                x = c_ref[:, s0 - SSD_WIDTH - 1024:s0 - SSD_WIDTH - 1024 + CONV_SEC]
            x = x.astype(F32)
            e = convp_ref[:, :, cs].reshape(q, CONV_SEC)
            acc = cb_ref[:, cs] + cw_ref[SSD_CONV - 1:SSD_CONV, cs] * x
            for d in range(1, SSD_CONV):
                prev = jnp.where(rowt >= d, pltpu.roll(x, d, axis=0), pltpu.roll(e, q - tl + d, axis=0))
                acc = acc + cw_ref[SSD_CONV - 1 - d:SSD_CONV - d, cs] * prev
            act[:, cs] = _silu(acc)

        dt = jax.nn.softplus(dt_ref[...] + dtb_ref[...])
        da = dt * (-jnp.exp(alog_ref[...]))
        a_cs = _seg_cumsum_rows(da, tl)
        a_tot = a_cs + _seg_revsum_rows(da, tl) - da
        a_cst, a_tott, dtt = a_cs.T, a_tot.T, dt.T
        ea = jnp.exp(a_cst)
        _to_channel_major(act, xt)

        s_i = lax.broadcasted_iota(jnp.int32, (q, q), 0)
        l_i = lax.broadcasted_iota(jnp.int32, (q, q), 1)
        causal = (l_i >= s_i) & (jnp.right_shift(l_i, 3) == jnp.right_shift(s_i, 3))
        for g in range(SSD_GROUPS):
            bg, cg = _group_bc(act, g)
            bgs[g] = bg
            cgs[g] = cg
            cbt = _dot_nt(bg, cg)
            for r in range(SSD_HPG):
                h = g * SSD_HPG + r
                hs = slice(h * hp, (h + 1) * hp)
                arow = a_cst[h:h + 1, :]
                acol = a_cs[:, h:h + 1]
                lt = jnp.where(causal, jnp.exp(arow - acol), 0.0)
                wt = (cbt * lt).astype(BF16)
                xh = xt[hs, :]
                xdt = xh * dtt[h:h + 1, :]
                yt[hs, :] = _dot(xdt.astype(BF16), wt) + d_ref[h] * xh
                xd[hs, :] = xdt * jnp.exp(a_tott[h:h + 1, :] - arow)
                ear[hs, :] = jnp.broadcast_to(ea[h:h + 1, :], (hp, q))

    lane_seq = jnp.right_shift(lax.broadcasted_iota(jnp.int32, (SSD_GW, q), 1), 3)

    def seq_body(sl, carry):
        sq = sub * SSD_SUB + sl
        inseq = lane_seq == sq
        dbase = (tile * SSD_PACK + sq) * SSD_HEADS
        for g in range(SSD_GROUPS):
            rows = slice(g * SSD_GW, (g + 1) * SSD_GW)
            stg = ssm0_ref[sl, rows, :]
            yo = _dot_nt(stg.astype(BF16), cgs[g])
            yt[rows, :] = yt[rows, :] + jnp.where(inseq, yo * ear[rows, :], 0.0)
            upd = _dot(jnp.where(inseq, xd[rows, :], 0.0).astype(BF16), bgs[g])
            for r in range(SSD_HPG):
                h = g * SSD_HPG + r
                ssmo_ref[sl, h * hp:(h + 1) * hp, :] = (dec_ref[dbase + h] * stg[r * hp:(r + 1) * hp, :]
                                                        + upd[r * hp:(r + 1) * hp, :])
        return carry

    lax.fori_loop(0, SSD_SUB, seq_body, 0)

    @pl.when(sub == SSD_PACK // SSD_SUB - 1)
    def _():
        _store_time_major(yt, y_ref)


def _ssd_packed(proj, dt, nseq, d_ssd, consts, conv0, ssm0):
    m = proj.shape[0]
    ntile = nseq // SSD_PACK
    nsub = SSD_PACK // SSD_SUB
    dec = _ssd_decay(dt, nseq, consts[2], consts[3])[:, :SSD_HEADS].reshape(-1)
    convp = jnp.pad(conv0, ((0, 0), (SSD_TOK - (SSD_CONV - 1), 0), (0, 0)))
    chan = pltpu.VMEM((SSD_WIDTH, LANES), F32)
    st_spec = pl.BlockSpec((SSD_SUB, SSD_WIDTH, SSD_STATE), lambda i, s: (i * nsub + s, 0, 0))
    return pl.pallas_call(
        _ssd_packed_body,
        grid=(ntile, nsub),
        in_specs=[pl.BlockSpec(memory_space=pltpu.SMEM), pl.BlockSpec(memory_space=pltpu.SMEM)]
        + _ssd_proj_specs(SSD_CHUNK, lambda i, s: i) + _ssd_const_specs()
        + [pl.BlockSpec((SSD_PACK, SSD_TOK, SSD_CONV_DIM), lambda i, s: (i, 0, 0)), st_spec],
        out_specs=[pl.BlockSpec((SSD_CHUNK, SSD_WIDTH), lambda i, s: (i, 0)), st_spec],
        out_shape=[jax.ShapeDtypeStruct((m, SSD_WIDTH), BF16),
                   jax.ShapeDtypeStruct((nseq, SSD_WIDTH, SSD_STATE), F32)],
        scratch_shapes=[pltpu.VMEM((SSD_CHUNK, SSD_CONV_DIM), F32), chan, chan, chan, chan,
                        pltpu.VMEM((SSD_GROUPS, SSD_CHUNK, SSD_STATE), BF16),
                        pltpu.VMEM((SSD_GROUPS, SSD_CHUNK, SSD_STATE), BF16)],
        compiler_params=_cparams(("arbitrary", "arbitrary")),
        name="ssd_packed",
    )(dec, d_ssd, proj, proj, proj, dt, *consts, convp, ssm0)


S5_GB = 8
S5_SCAN_SEQS = 2


def _s5_body(*refs, q, scan):
    if scan:
        u_ref, toep_ref, bin_ref, coff_ref, aqr_ref, aqi_ref, y_ref, so_ref, sr_s, si_s = refs
        s0_ref = None
    else:
        u_ref, toep_ref, bin_ref, coff_ref, aqr_ref, aqi_ref, s0_ref, y_ref, so_ref = refs
    k = S5_GROUP_SIZE
    p = S5_STATE
    gp = S5_GB * p
    w = u_ref.shape[2]
    nsq = w // LANES

    def chunk_in(gl):
        return u_ref[:, gl * k:(gl + 1) * k, :].reshape(q * k, w)

    if scan:
        for gl in range(S5_GB):
            bo = _dot(bin_ref[gl], chunk_in(gl).astype(BF16))
            sr_s[gl * p:(gl + 1) * p, :] = bo[:p, :]
            si_s[gl * p:(gl + 1) * p, :] = bo[p:, :]
        lane = lax.broadcasted_iota(jnp.int32, (gp, w), 1) & (LANES - 1)
        xr, xi = sr_s[...], si_s[...]
        pr = jnp.concatenate([aqr_ref[...].reshape(gp, LANES)] * nsq, axis=1)
        pi = jnp.concatenate([aqi_ref[...].reshape(gp, LANES)] * nsq, axis=1)
        s = 1
        while s < LANES:
            m = lane >= s
            tr = jnp.where(m, pltpu.roll(xr, s, axis=1), 0.0)
            ti = jnp.where(m, pltpu.roll(xi, s, axis=1), 0.0)
            xr, xi = xr + (pr * tr - pi * ti), xi + (pr * ti + pi * tr)
            if 2 * s < LANES:
                pr, pi = pr * pr - pi * pi, 2.0 * pr * pi
            s *= 2
        for sq in range(nsq):
            last = slice((sq + 1) * LANES - 1, (sq + 1) * LANES)
            so_ref[sq, :, :p, :] = xr[:, last].reshape(S5_GB, p, 1)
            so_ref[sq, :, p:, :] = xi[:, last].reshape(S5_GB, p, 1)
        m1 = lane >= 1
        sr_s[...] = jnp.where(m1, pltpu.roll(xr, 1, axis=1), 0.0)
        si_s[...] = jnp.where(m1, pltpu.roll(xi, 1, axis=1), 0.0)

    for gl in range(S5_GB):
        z = chunk_in(gl)
        if scan:
            sin_r, sin_i = sr_s[gl * p:(gl + 1) * p, :], si_s[gl * p:(gl + 1) * p, :]
        else:
            bo = _dot_hi(bin_ref[gl], z)
            aqr, aqi = aqr_ref[gl], aqi_ref[gl]
            sin_r, sin_i = s0_ref[gl, :p, :], s0_ref[gl, p:, :]
            so_ref[gl, :p, :] = aqr * sin_r - aqi * sin_i + bo[:p, :]
            so_ref[gl, p:, :] = aqr * sin_i + aqi * sin_r + bo[p:, :]
        s_in = jnp.concatenate([sin_r, sin_i], axis=0).astype(BF16)
        y = _dot(toep_ref[gl], z.astype(BF16)) + _dot(coff_ref[gl], s_in)
        y_ref[:, gl * k:(gl + 1) * k, :] = y.reshape(q, k, w).astype(y_ref.dtype)


def _s5(ut, q, scan, prep, s0t=None):
    ctot = ut.shape[2]
    lw = min(S5_SCAN_SEQS * LANES, ctot) if scan else LANES
    nlb = ctot // lw
    ngb = S5_GROUPS // S5_GB
    qk = q * S5_GROUP_SIZE
    gspec = lambda r, c, cb=0: pl.BlockSpec((S5_GB, r, c), lambda gb, lb: (gb, 0, cb))
    in_specs = [pl.BlockSpec((q, S5_GB * S5_GROUP_SIZE, lw), lambda gb, lb: (0, gb, lb)),
                gspec(qk, qk), gspec(2 * S5_STATE, qk, (S5_Q - q) * S5_GROUP_SIZE // qk),
                gspec(qk, 2 * S5_STATE), gspec(S5_STATE, LANES), gspec(S5_STATE, LANES)]
    args = [ut, prep["toep"], prep["binb"] if scan else prep["bin"], prep["coff"], *prep["aq"][q]]
    scratch = []
    if scan:
        so_shape = jax.ShapeDtypeStruct((ctot // LANES, S5_GROUPS, 2 * S5_STATE, 1), F32)
        so_spec = pl.BlockSpec((lw // LANES, S5_GB, 2 * S5_STATE, 1), lambda gb, lb: (lb, gb, 0, 0))
        scratch = [pltpu.VMEM((S5_GB * S5_STATE, lw), F32)] * 2
    else:
        in_specs.append(gspec(2 * S5_STATE, LANES))
        args.append(s0t)
        so_shape = jax.ShapeDtypeStruct((S5_GROUPS, 2 * S5_STATE, LANES), F32)
        so_spec = gspec(2 * S5_STATE, LANES)
    return pl.pallas_call(
        functools.partial(_s5_body, q=q, scan=scan),
        grid=(ngb, nlb),
        in_specs=in_specs,
        out_specs=[pl.BlockSpec((q, S5_GB * S5_GROUP_SIZE, lw), lambda gb, lb: (0, gb, lb)), so_spec],
        out_shape=[jax.ShapeDtypeStruct(ut.shape, BF16), so_shape],
        scratch_shapes=scratch,
        compiler_params=_cparams(("arbitrary", "arbitrary")),
        name="s5_scan" if scan else "s5_step",
    )(*args)


OUT_ROWS = 512


def _out_body(y5_ref, u5_ref, z5_ref, ga_ref, gb_ref, ys_ref, zs_ref, x_ref, d5_ref, wglu_ref, bglu_ref,
              wd5_ref, nws_ref, wda_ref, wout_ref, fnw_ref, o_ref):
    f32 = lambda r: r[...].astype(F32)
    y5 = jax.nn.gelu(f32(y5_ref) + d5_ref[...] * f32(u5_ref))
    y5 = y5 * _sigmoid(_dot(y5.astype(BF16), wglu_ref[...]) + bglu_ref[...])
    y5 = y5 * _silu(f32(z5_ref))
    y_b = _dot(y5.astype(BF16), wd5_ref[...])
    y_a = None
    for g in range(SSD_GROUPS):
        sl = slice(g * SSD_GW, (g + 1) * SSD_GW)
        v = ys_ref[:, sl].astype(F32) * _silu(zs_ref[:, sl].astype(F32))
        ms = jnp.mean(v * v, axis=-1, keepdims=True)
        yz = (v * lax.rsqrt(ms + NORM_EPS) * nws_ref[:, sl]).astype(BF16)
        part = _dot(yz, wda_ref[sl, :])
        y_a = part if y_a is None else y_a + part
    mix = _sigmoid(f32(ga_ref)) * y_a + _sigmoid(f32(gb_ref)) * y_b
    out = x_ref[...] + _dot(mix.astype(BF16), wout_ref[...])
    ms = jnp.mean(out * out, axis=-1, keepdims=True)
    o_ref[...] = out * lax.rsqrt(ms + NORM_EPS) * fnw_ref[...]


def _out(y5, u5, proj, ys, x2d, prm):
    m = x2d.shape[0]
    rowb = lambda w, cb: pl.BlockSpec((OUT_ROWS, w), lambda i: (i, cb))
    const = lambda r, c: pl.BlockSpec((r, c), lambda i: (0, 0))
    return pl.pallas_call(
        _out_body,
        grid=(m // OUT_ROWS,),
        in_specs=[rowb(1024, 0), rowb(1024, 0), rowb(1024, CB_Z5), rowb(1024, CB_GA), rowb(1024, CB_GB),
                  rowb(SSD_WIDTH, 0), rowb(SSD_WIDTH, 0), rowb(1024, 0),
                  const(1, 1024), const(1024, 1024), const(1, 1024), const(1024, 1024),
                  const(1, SSD_WIDTH), const(SSD_WIDTH, 1024), const(1024, 1024), const(1, 1024)],
        out_specs=rowb(1024, 0),
        out_shape=jax.ShapeDtypeStruct((m, D_MODEL), F32),
        compiler_params=_cparams(("arbitrary",)),
        name="out_stage",
    )(y5, u5, proj, proj, proj, ys, proj, x2d, prm["d_s5"], prm["w_glu"], prm["b_glu"],
      prm["w_down_s5"], prm["ssd_norm_w"], prm["w_down_ssd"], prm["w_out"], prm["final_norm_w"])


def _run(x, conv0, ssm0, re0, im0, prm):
    nseq, seqlen, _ = x.shape
    m = nseq * seqlen
    prompt = conv0 is None
    x2d = x.reshape(m, D_MODEL)
    proj, dt, u5 = _in_proj(x2d, prm["norm_w"], prm["w_cat"], prm["w_dt"])
    conv_new = proj.reshape(nseq, seqlen, PROJ_COLS)[:, seqlen - (SSD_CONV - 1):,
                                                     SSD_WIDTH:SSD_WIDTH + SSD_CONV_DIM].astype(F32)

    if prompt:
        yz, ssm_new = _ssd(proj, dt, nseq, seqlen // SSD_CHUNK, prm["d_ssd"], prm["ssd_consts"])
    else:
        yz, ssm_new = _ssd_packed(proj, dt, nseq, prm["d_ssd"], prm["ssd_consts"], conv0,
                                  ssm0.reshape(nseq, SSD_WIDTH, SSD_STATE))

    q = S5_Q if prompt else seqlen
    ut = jnp.transpose(u5.reshape(m // q, q, S5_WIDTH), (1, 2, 0))
    if prompt:
        yt, so = _s5(ut, q, True, prm["s5"])
        so = so.reshape(nseq, S5_GROUPS, 2, S5_STATE)
        re_new, im_new = so[:, :, 0, :], so[:, :, 1, :]
    else:
        s0t = jnp.transpose(jnp.concatenate([re0, im0], axis=-1), (1, 2, 0))
        yt, so = _s5(ut, q, False, prm["s5"], s0t)
        so = jnp.transpose(so, (2, 0, 1))
        re_new, im_new = so[:, :, :S5_STATE], so[:, :, S5_STATE:]
    y5 = jnp.transpose(yt, (2, 0, 1)).reshape(m, S5_WIDTH)

    out = _out(y5, u5, proj, yz, x2d, prm)
    return (out.reshape(nseq, seqlen, D_MODEL), conv_new[None],
            ssm_new.reshape(1, nseq, SSD_HEADS, SSD_HEAD_DIM, SSD_STATE), re_new[None], im_new[None])


def kernel(x_prompt, x_sample, state_conv, state_ssm, state_s5_re, state_s5_im, norm_w, w_in, conv_w, conv_b,
           dt_bias, A_log, D_ssd, ssd_norm_w, w_down_ssd, lam_re, lam_im, log_dt, B_re, B_im, C_re, C_im, D_s5,
           w_glu, b_glu, w_down_s5, w_out, final_norm_w):
    assert norm_w.shape[0] == 1, "single-layer trunk"
    assert x_prompt.shape[1] % SSD_CHUNK == 0 and x_prompt.shape[1] // S5_Q == LANES
    assert x_sample.shape[1] == SSD_TOK and x_sample.shape[0] == LANES
    w = w_in[0]
    o_dt = SSD_WIDTH + SSD_CONV_DIM
    w_cat = jnp.concatenate([w[:, :o_dt], w[:, o_dt + SSD_HEADS:]], axis=1).astype(BF16)
    w_dt = jnp.pad(w[:, o_dt:o_dt + SSD_HEADS], ((0, 0), (0, LANES - SSD_HEADS))).astype(BF16)
    prm = dict(norm_w=norm_w[0].reshape(1, -1), w_cat=w_cat, w_dt=w_dt,
               ssd_consts=_ssd_consts(conv_w[0], conv_b[0], dt_bias[0], A_log[0]), d_ssd=D_ssd[0],
               ssd_norm_w=ssd_norm_w[0].reshape(1, -1),
               s5=_s5_prep(lam_re[0], lam_im[0], log_dt[0], B_re[0], B_im[0], C_re[0], C_im[0]),
               d_s5=D_s5[0].reshape(1, -1), w_glu=w_glu[0].astype(BF16), b_glu=b_glu[0].reshape(1, -1),
               w_down_s5=w_down_s5[0].astype(BF16), w_down_ssd=w_down_ssd[0].astype(BF16),
               w_out=w_out[0].astype(BF16), final_norm_w=final_norm_w.reshape(1, -1))

    y_p, conv_p, ssm_p, re_p, im_p = _run(x_prompt, None, None, None, None, prm)
    y_s, conv_s, ssm_s, re_s, im_s = _run(x_sample, state_conv[0], state_ssm[0], state_s5_re[0],
                                          state_s5_im[0], prm)
    return (y_p, y_s, conv_p, ssm_p, re_p, im_p, conv_s, ssm_s, re_s, im_s)
```

```python
import functools

import jax
import jax.numpy as jnp
from jax import lax
from jax.experimental import pallas as pl
from jax.experimental.pallas import tpu as pltpu

F32 = jnp.float32
BF16 = jnp.bfloat16

D_MODEL = 1024
SSD_WIDTH = 2048
SSD_HEAD_DIM = 64
SSD_HEADS = 32
SSD_GROUPS = 8
SSD_HPG = SSD_HEADS // SSD_GROUPS
SSD_GW = SSD_WIDTH // SSD_GROUPS
SSD_STATE = 128
SSD_CONV = 4
SSD_CHUNK = 128
SSD_CONV_DIM = 4096
S5_WIDTH = 1024
S5_GROUP_SIZE = 16
S5_GROUPS = 64
S5_STATE = 64
NORM_EPS = 1e-6
LOG2E = 1.4426950408889634

LANES = 128
SUBLANES = 8

PROJ_COLS = 10240
COL_TILE = 1024
ROW_TILE = 2048
CB_B, CB_C, CB_U5, CB_Z5, CB_GA, CB_GB = 4, 5, 6, 7, 8, 9

VMEM_LIMIT = 56 * 1024 * 1024


def _cparams(sem, flags=None):
    return pltpu.CompilerParams(dimension_semantics=sem, vmem_limit_bytes=VMEM_LIMIT, flags=flags)


def _sigmoid(x):
    return 0.5 * jnp.tanh(0.5 * x) + 0.5


def _silu(x):
    h = 0.5 * x
    return h * jnp.tanh(h) + h


def _dot(a, b):
    return jnp.dot(a, b, preferred_element_type=F32)


def _dot_nt(a, b):
    return lax.dot_general(a, b, (((1,), (1,)), ((), ())), preferred_element_type=F32)


def _dot_hi(a, b):
    a_hi = a.astype(BF16)
    a_lo = (a - a_hi.astype(F32)).astype(BF16)
    if b.dtype == BF16:
        return _dot(a_hi, b) + _dot(a_lo, b)
    b_hi = b.astype(BF16)
    b_lo = (b - b_hi.astype(F32)).astype(BF16)
    return _dot(a_hi, b_hi) + _dot(a_hi, b_lo) + _dot(a_lo, b_hi)


S5_Q = 16
PREP_GB = 8


def _zoh(lr, li, ldt):
    step = jnp.exp(ldt)
    mag = jnp.exp(lr * step)
    return mag * jnp.cos(li * step), mag * jnp.sin(li * step)


def _s5_prep_body(rowp_ref, cre_ref, cim_ref, b1_ref, b2_ref,
                  toep_ref, coff_ref, bin_ref, binb_ref, aq16r_ref, aq16i_ref, aq8r_ref, aq8i_ref, ca_s):
    k, p, q = S5_GROUP_SIZE, S5_STATE, S5_Q
    qk = q * k
    lr, li = rowp_ref[:, 0:1, :], rowp_ref[:, 1:2, :]
    ar_l, ai_l = _zoh(lr, li, rowp_ref[:, 2:3, :])
    den = lr * lr + li * li
    nr = ar_l - 1.0
    cr_l = (nr * lr + ai_l * li) / den
    ci_l = (ai_l * lr - nr * li) / den

    def on_sublanes(v):
        return jnp.stack([jnp.broadcast_to(v[g], (2 * p, 2 * p)).T for g in range(PREP_GB)])

    wide = lambda v: jnp.concatenate([v] * (qk // (2 * p)), axis=-1)

    ar, ai, cr, ci = on_sublanes(ar_l), on_sublanes(ai_l), on_sublanes(cr_l), on_sublanes(ci_l)
    sgn_row = jnp.where(lax.broadcasted_iota(jnp.int32, (PREP_GB, 2 * p, qk), 1) < p, -1.0, 1.0)
    b1, b2 = b1_ref[...], b2_ref[...]
    crw, ciw = wide(cr), sgn_row * wide(ci)
    bb = crw * b1 + ciw * b2
    bb_sw = crw * b2 - ciw * b1
    lane_t = jnp.right_shift(lax.broadcasted_iota(jnp.int32, (PREP_GB, 2 * p, qk), 2), 4)
    pr, pi = jnp.ones_like(ar), jnp.zeros_like(ar)
    pwr = jnp.zeros((PREP_GB, 2 * p, qk), F32)
    pwi = jnp.zeros((PREP_GB, 2 * p, qk), F32)
    for j in range(q):
        sel = lane_t == (q - 1 - j)
        pwr = jnp.where(sel, wide(pr), pwr)
        pwi = jnp.where(sel, wide(pi), pwi)
        if j == q // 2:
            aq8r_ref[...] = pr[:, :p, :]
            aq8i_ref[...] = pi[:, :p, :]
        pr, pi = pr * ar - pi * ai, pr * ai + pi * ar
    binv = pwr * bb + sgn_row * pwi * bb_sw
    bin_ref[...] = binv
    binb_ref[...] = binv.astype(BF16)

    ar, ai = ar_l, ai_l
    low = lax.broadcasted_iota(jnp.int32, (PREP_GB, 1, 2 * p), 2) < p
    sgn_lane = jnp.where(low, 1.0, -1.0)
    cre, cim = cre_ref[...], cim_ref[...]
    pr, pi = jnp.ones_like(ar), jnp.zeros_like(ar)
    for j in range(q + 1):
        p1 = jnp.where(low, pr, pi)
        p2 = jnp.where(low, pi, pr)
        ca_s[:, j * k:(j + 1) * k, :] = sgn_lane * cre * p1 - cim * p2
        if j == q:
            aq16r_ref[...] = pr
            aq16i_ref[...] = pi
        pr, pi = pr * ar - pi * ai, pr * ai + pi * ar
    coff_ref[...] = ca_s[:, k:, :].astype(BF16)

    lane_blk = jnp.right_shift(lax.broadcasted_iota(jnp.int32, (qk, qk), 1), 4)
    for g in range(PREP_GB):
        km = _dot_hi(ca_s[g, :qk, :], bb[g])
        tp = jnp.where(lane_blk == 0, km, 0.0)
        for t in range(1, q):
            sh = jnp.concatenate([jnp.zeros((t * k, qk), F32), km[:qk - t * k, :]], axis=0)
            tp = jnp.where(lane_blk == t, sh, tp)
        toep_ref[g] = tp.astype(BF16)


def _s5_prep(lam_re, lam_im, log_dt, b_re, b_im, c_re, c_im):
    g, p, k, q = S5_GROUPS, S5_STATE, S5_GROUP_SIZE, S5_Q
    qk = q * k
    dup = lambda a: jnp.concatenate([a, a], axis=-1)
    ldt = jnp.broadcast_to(log_dt[:, None], (g, 2 * p))
    rowp = jnp.stack([dup(lam_re), dup(lam_im), ldt], axis=1)
    b1 = jnp.tile(jnp.concatenate([b_re, b_im], axis=1), (1, 1, q))
    b2 = jnp.tile(jnp.concatenate([b_im, b_re], axis=1), (1, 1, q))
    blk = lambda r, c: pl.BlockSpec((PREP_GB, r, c), lambda i: (i, 0, 0))
    shp = lambda r, c, dt: jax.ShapeDtypeStruct((g, r, c), dt)
    toep, coff, bin_, binb, aq16r, aq16i, aq8r, aq8i = pl.pallas_call(
        _s5_prep_body,
        grid=(g // PREP_GB,),
        in_specs=[blk(3, 2 * p), blk(k, 2 * p), blk(k, 2 * p), blk(2 * p, qk), blk(2 * p, qk)],
        out_specs=[blk(qk, qk), blk(qk, 2 * p), blk(2 * p, qk), blk(2 * p, qk),
                   blk(1, 2 * p), blk(1, 2 * p), blk(p, LANES), blk(p, LANES)],
        out_shape=[shp(qk, qk, BF16), shp(qk, 2 * p, BF16), shp(2 * p, qk, F32), shp(2 * p, qk, BF16),
                   shp(1, 2 * p, F32), shp(1, 2 * p, F32), shp(p, LANES, F32), shp(p, LANES, F32)],
        scratch_shapes=[pltpu.VMEM((PREP_GB, (q + 1) * k, 2 * p), F32)],
        compiler_params=_cparams(("arbitrary",)),
        name="s5_prep",
    )(rowp, dup(c_re), dup(c_im), b1, b2)
    return dict(toep=toep, coff=coff, bin=bin_, binb=binb,
                aq={16: (aq16r, aq16i), 8: (aq8r, aq8i)})


def _in_proj_body(x_ref, nw_ref, w_ref, wdt_ref, o_ref, dt_ref, u5_ref, h_ref):
    j = pl.program_id(1)

    @pl.when(j == 0)
    def _():
        x = x_ref[...]
        ms = jnp.mean(x * x, axis=-1, keepdims=True)
        h = (x * lax.rsqrt(ms + NORM_EPS) * nw_ref[...]).astype(BF16)
        h_ref[...] = h
        dt_ref[...] = _dot(h, wdt_ref[...])

    o = _dot(h_ref[...], w_ref[...]).astype(BF16)
    o_ref[...] = o

    @pl.when(j == CB_U5)
    def _():
        u5_ref[...] = o


def _in_proj(x2d, norm_w, w_cat, w_dt):
    m = x2d.shape[0]
    rt = min(ROW_TILE, m)
    return pl.pallas_call(
        _in_proj_body,
        grid=(m // rt, PROJ_COLS // COL_TILE),
        in_specs=[pl.BlockSpec((rt, D_MODEL), lambda i, j: (i, 0)),
                  pl.BlockSpec((1, D_MODEL), lambda i, j: (0, 0)),
                  pl.BlockSpec((D_MODEL, COL_TILE), lambda i, j: (0, j)),
                  pl.BlockSpec((D_MODEL, LANES), lambda i, j: (0, 0))],
        out_specs=[pl.BlockSpec((rt, COL_TILE), lambda i, j: (i, j)),
                   pl.BlockSpec((rt, LANES), lambda i, j: (i, 0)),
                   pl.BlockSpec((rt, COL_TILE), lambda i, j: (i, 0))],
        out_shape=[jax.ShapeDtypeStruct((m, PROJ_COLS), BF16), jax.ShapeDtypeStruct((m, LANES), F32),
                   jax.ShapeDtypeStruct((m, S5_WIDTH), BF16)],
        scratch_shapes=[pltpu.VMEM((rt, D_MODEL), BF16)],
        compiler_params=_cparams(("arbitrary", "arbitrary")),
        name="in_proj",
    )(x2d, norm_w, w_cat, w_dt)


CONV_PAD = SUBLANES
CONV_SEC = 512


def _seg_cumsum_rows(x, seg):
    t = lax.broadcasted_iota(jnp.int32, x.shape, 0) & (seg - 1)
    s = 1
    while s < seg:
        x = x + jnp.where(t >= s, pltpu.roll(x, s, axis=0), 0.0)
        s *= 2
    return x


def _seg_revsum_rows(x, seg):
    n = x.shape[0]
    t = lax.broadcasted_iota(jnp.int32, x.shape, 0) & (seg - 1)
    s = 1
    while s < seg:
        x = x + jnp.where(t + s < seg, pltpu.roll(x, n - s, axis=0), 0.0)
        s *= 2
    return x


def _to_channel_major(src, dst):
    for j in range(SSD_WIDTH // LANES):
        sl = slice(j * LANES, (j + 1) * LANES)
        dst[sl, :] = src[:, sl].astype(F32).T


def _store_time_major(yt, y_ref):
    for j in range(SSD_WIDTH // LANES):
        sl = slice(j * LANES, (j + 1) * LANES)
        y_ref[:, sl] = yt[sl, :].T.astype(y_ref.dtype)


def _group_bc(act, g):
    bg = act[:, SSD_WIDTH + g * SSD_STATE:SSD_WIDTH + (g + 1) * SSD_STATE].astype(BF16)
    cg = act[:, SSD_WIDTH + 1024 + g * SSD_STATE:SSD_WIDTH + 1024 + (g + 1) * SSD_STATE].astype(BF16)
    return bg, cg


SSD_CPS = 2


def _ssd_body(d_ref, xs_ref, b_ref, c_ref, dt_ref, cw_ref, cb_ref, dtb_ref, alog_ref,
              y_ref, ssmo_ref, cbuf, act, xt, yt, state):
    q = SSD_CHUNK
    hp = SSD_HEAD_DIM

    k1 = SSD_CONV - 1
    base = CONV_PAD

    @pl.when(pl.program_id(1) == 0)
    def _():
        state[...] = jnp.zeros_like(state)
        cbuf[base - k1:base, :] = jnp.zeros((k1, SSD_CONV_DIM), F32)

    causal = lax.broadcasted_iota(jnp.int32, (q, q), 1) >= lax.broadcasted_iota(jnp.int32, (q, q), 0)

    def chunk(ci, carry):
        rows = pl.ds(pl.multiple_of(ci * q, q), q)
        cbuf[base:, 0:SSD_WIDTH] = xs_ref[rows, :].astype(F32)
        cbuf[base:, SSD_WIDTH:SSD_WIDTH + 1024] = b_ref[rows, :].astype(F32)
        cbuf[base:, SSD_WIDTH + 1024:] = c_ref[rows, :].astype(F32)
        for s0 in range(0, SSD_CONV_DIM, CONV_SEC):
            cs = slice(s0, s0 + CONV_SEC)
            acc = cb_ref[:, cs]
            for j in range(SSD_CONV):
                acc = acc + cw_ref[j:j + 1, cs] * cbuf[base - k1 + j:base - k1 + j + q, cs]
            act[:, cs] = _silu(acc)
        cbuf[base - k1:base, :] = cbuf[base + q - k1:base + q, :]

        dt = jax.nn.softplus(dt_ref[rows, :] + dtb_ref[...])
        a_cs = _seg_cumsum_rows(dt * (-jnp.exp(alog_ref[...])), q) * LOG2E
        a_cst = a_cs.T
        dtt = dt.T
        _to_channel_major(act, xt)

        for g in range(SSD_GROUPS):
            bg, cg = _group_bc(act, g)
            cbt = _dot_nt(bg, cg)
            stg = state[g * SSD_GW:(g + 1) * SSD_GW, :]
            yoff = _dot_nt(stg.astype(BF16), cg)
            for r in range(SSD_HPG):
                h = g * SSD_HPG + r
                hs = slice(h * hp, (h + 1) * hp)
                arow = a_cst[h:h + 1, :]
                acol = a_cs[:, h:h + 1]
                lt = jnp.where(causal, jnp.exp2(arow - acol), 0.0)
                wt = (cbt * lt).astype(BF16)
                xh = xt[hs, :]
                xdt = xh * dtt[h:h + 1, :]
                yt[hs, :] = (_dot(xdt.astype(BF16), wt) + yoff[r * hp:(r + 1) * hp, :] * jnp.exp2(arow)
                             + d_ref[h] * xh)
                atot = arow[:, q - 1:q]
                xdec = xdt * jnp.exp2(atot - arow)
                state[hs, :] = jnp.exp2(atot) * stg[r * hp:(r + 1) * hp, :] + _dot(xdec.astype(BF16), bg)

        _store_time_major(yt, y_ref.at[rows, :])
        return carry

    lax.fori_loop(0, SSD_CPS, chunk, 0)
    ssmo_ref[...] = state[...]


def _ssd_consts(conv_w, conv_b, dt_bias, a_log):
    padl = lambda v: jnp.pad(v, (0, LANES - SSD_HEADS)).reshape(1, LANES)
    return (conv_w, conv_b.reshape(1, -1), padl(dt_bias), padl(a_log))


def _ssd_const_specs():
    const = lambda shape: pl.BlockSpec(shape, lambda *_: (0, 0))
    return [const((SSD_CONV, SSD_CONV_DIM)), const((1, SSD_CONV_DIM)), const((1, LANES)), const((1, LANES))]


def _ssd_proj_specs(rows, rowf):
    return [pl.BlockSpec((rows, SSD_WIDTH), lambda *a: (rowf(*a), 1)),
            pl.BlockSpec((rows, 1024), lambda *a: (rowf(*a), CB_B)),
            pl.BlockSpec((rows, 1024), lambda *a: (rowf(*a), CB_C)),
            pl.BlockSpec((rows, LANES), lambda *a: (rowf(*a), 0))]


def _ssd(proj, dt, nseq, nchunk, d_ssd, consts):
    m = proj.shape[0]
    nstep = nchunk // SSD_CPS
    rows = SSD_CPS * SSD_CHUNK
    rowf = lambda b, c: b * nstep + c
    chan = pltpu.VMEM((SSD_WIDTH, LANES), F32)
    return pl.pallas_call(
        _ssd_body,
        grid=(nseq, nstep),
        in_specs=[pl.BlockSpec(memory_space=pltpu.SMEM)] + _ssd_proj_specs(rows, rowf) + _ssd_const_specs(),
        out_specs=[pl.BlockSpec((rows, SSD_WIDTH), lambda b, c: (rowf(b, c), 0)),
                   pl.BlockSpec((None, SSD_WIDTH, SSD_STATE), lambda b, c: (b, 0, 0))],
        out_shape=[jax.ShapeDtypeStruct((m, SSD_WIDTH), BF16),
                   jax.ShapeDtypeStruct((nseq, SSD_WIDTH, SSD_STATE), F32)],
        scratch_shapes=[pltpu.VMEM((CONV_PAD + SSD_CHUNK, SSD_CONV_DIM), F32),
                        pltpu.VMEM((SSD_CHUNK, SSD_CONV_DIM), F32),
                        chan, chan, pltpu.VMEM((SSD_WIDTH, SSD_STATE), F32)],
        compiler_params=_cparams(("arbitrary", "arbitrary")),
        name="ssd",
    )(d_ssd, proj, proj, proj, dt, *consts)


SSD_PACK = 16
SSD_TOK = SSD_CHUNK // SSD_PACK
SSD_SUB = 8


def _ssd_decay_body(dt_ref, dtb_ref, alog_ref, o_ref):
    da = jax.nn.softplus(dt_ref[...] + dtb_ref[...]) * (-jnp.exp(alog_ref[...]))
    o_ref[...] = jnp.exp(jnp.sum(da.reshape(o_ref.shape[0], SSD_TOK, LANES), axis=1))


def _ssd_decay(dt, nseq, dtb, alog):
    const = lambda shape: pl.BlockSpec(shape, lambda i: (0, 0))
    return pl.pallas_call(
        _ssd_decay_body,
        grid=(1,),
        in_specs=[const((nseq * SSD_TOK, LANES)), const((1, LANES)), const((1, LANES))],
        out_specs=const((nseq, LANES)),
        out_shape=jax.ShapeDtypeStruct((nseq, LANES), F32),
        compiler_params=_cparams(("arbitrary",)),
        name="ssd_decay",
    )(dt, dtb, alog)


def _ssd_packed_body(dec_ref, d_ref, xs_ref, b_ref, c_ref, dt_ref, cw_ref, cb_ref, dtb_ref, alog_ref,
                     convp_ref, ssm0_ref,
                     y_ref, ssmo_ref, act, xt, yt, xd, ear, bgs, cgs):
    q = SSD_CHUNK
    tl = SSD_TOK
    hp = SSD_HEAD_DIM
    tile = pl.program_id(0)
    sub = pl.program_id(1)

    @pl.when(sub == 0)
    def _shared():
        rowt = lax.broadcasted_iota(jnp.int32, (q, CONV_SEC), 0) & (tl - 1)
        for s0 in range(0, SSD_CONV_DIM, CONV_SEC):
            cs = slice(s0, s0 + CONV_SEC)
            if s0 < SSD_WIDTH:
                x = xs_ref[:, cs]
            elif s0 < SSD_WIDTH + 1024:
                x = b_ref[:, s0 - SSD_WIDTH:s0 - SSD_WIDTH + CONV_SEC]
            else:
                x = c_ref[:, s0 - SSD_WIDTH - 1024:s0 - SSD_WIDTH - 1024 + CONV_SEC]
            x = x.astype(F32)
            e = convp_ref[:, :, cs].reshape(q, CONV_SEC)
            acc = cb_ref[:, cs] + cw_ref[SSD_CONV - 1:SSD_CONV, cs] * x
            for d in range(1, SSD_CONV):
                prev = jnp.where(rowt >= d, pltpu.roll(x, d, axis=0), pltpu.roll(e, q - tl + d, axis=0))
                acc = acc + cw_ref[SSD_CONV - 1 - d:SSD_CONV - d, cs] * prev
            act[:, cs] = _silu(acc)

        dt = jax.nn.softplus(dt_ref[...] + dtb_ref[...])
        da = dt * (-jnp.exp(alog_ref[...]))
        a_cs = _seg_cumsum_rows(da, tl)
        a_tot = a_cs + _seg_revsum_rows(da, tl) - da
        a_cst, a_tott, dtt = a_cs.T, a_tot.T, dt.T
        ea = jnp.exp(a_cst)
        _to_channel_major(act, xt)

        s_i = lax.broadcasted_iota(jnp.int32, (q, q), 0)
        l_i = lax.broadcasted_iota(jnp.int32, (q, q), 1)
        causal = (l_i >= s_i) & (jnp.right_shift(l_i, 3) == jnp.right_shift(s_i, 3))
        for g in range(SSD_GROUPS):
            bg, cg = _group_bc(act, g)
            bgs[g] = bg
            cgs[g] = cg
            cbt = _dot_nt(bg, cg)
            for r in range(SSD_HPG):
                h = g * SSD_HPG + r
                hs = slice(h * hp, (h + 1) * hp)
                arow = a_cst[h:h + 1, :]
                acol = a_cs[:, h:h + 1]
                lt = jnp.where(causal, jnp.exp(arow - acol), 0.0)
                wt = (cbt * lt).astype(BF16)
                xh = xt[hs, :]
                xdt = xh * dtt[h:h + 1, :]
                yt[hs, :] = _dot(xdt.astype(BF16), wt) + d_ref[h] * xh
                xd[hs, :] = xdt * jnp.exp(a_tott[h:h + 1, :] - arow)
                ear[hs, :] = jnp.broadcast_to(ea[h:h + 1, :], (hp, q))

    lane_seq = jnp.right_shift(lax.broadcasted_iota(jnp.int32, (SSD_GW, q), 1), 3)

    def seq_body(sl, carry):
        sq = sub * SSD_SUB + sl
        inseq = lane_seq == sq
        dbase = (tile * SSD_PACK + sq) * SSD_HEADS
        for g in range(SSD_GROUPS):
            rows = slice(g * SSD_GW, (g + 1) * SSD_GW)
            stg = ssm0_ref[sl, rows, :]
            yo = _dot_nt(stg.astype(BF16), cgs[g])
            yt[rows, :] = yt[rows, :] + jnp.where(inseq, yo * ear[rows, :], 0.0)
            upd = _dot(jnp.where(inseq, xd[rows, :], 0.0).astype(BF16), bgs[g])
            for r in range(SSD_HPG):
                h = g * SSD_HPG + r
                ssmo_ref[sl, h * hp:(h + 1) * hp, :] = (dec_ref[dbase + h] * stg[r * hp:(r + 1) * hp, :]
                                                        + upd[r * hp:(r + 1) * hp, :])
        return carry

    lax.fori_loop(0, SSD_SUB, seq_body, 0)

    @pl.when(sub == SSD_PACK // SSD_SUB - 1)
    def _():
        _store_time_major(yt, y_ref)


def _ssd_packed(proj, dt, nseq, d_ssd, consts, conv0, ssm0):
    m = proj.shape[0]
    ntile = nseq // SSD_PACK
    nsub = SSD_PACK // SSD_SUB
    dec = _ssd_decay(dt, nseq, consts[2], consts[3])[:, :SSD_HEADS].reshape(-1)
    convp = jnp.pad(conv0, ((0, 0), (SSD_TOK - (SSD_CONV - 1), 0), (0, 0)))
    chan = pltpu.VMEM((SSD_WIDTH, LANES), F32)
    st_spec = pl.BlockSpec((SSD_SUB, SSD_WIDTH, SSD_STATE), lambda i, s: (i * nsub + s, 0, 0))
    return pl.pallas_call(
        _ssd_packed_body,
        grid=(ntile, nsub),
        in_specs=[pl.BlockSpec(memory_space=pltpu.SMEM), pl.BlockSpec(memory_space=pltpu.SMEM)]
        + _ssd_proj_specs(SSD_CHUNK, lambda i, s: i) + _ssd_const_specs()
        + [pl.BlockSpec((SSD_PACK, SSD_TOK, SSD_CONV_DIM), lambda i, s: (i, 0, 0)), st_spec],
        out_specs=[pl.BlockSpec((SSD_CHUNK, SSD_WIDTH), lambda i, s: (i, 0)), st_spec],
        out_shape=[jax.ShapeDtypeStruct((m, SSD_WIDTH), BF16),
                   jax.ShapeDtypeStruct((nseq, SSD_WIDTH, SSD_STATE), F32)],
        scratch_shapes=[pltpu.VMEM((SSD_CHUNK, SSD_CONV_DIM), F32), chan, chan, chan, chan,
                        pltpu.VMEM((SSD_GROUPS, SSD_CHUNK, SSD_STATE), BF16),
                        pltpu.VMEM((SSD_GROUPS, SSD_CHUNK, SSD_STATE), BF16)],
        compiler_params=_cparams(("arbitrary", "arbitrary")),
        name="ssd_packed",
    )(dec, d_ssd, proj, proj, proj, dt, *consts, convp, ssm0)


S5_GB = 8
S5_SCAN_SEQS = 2


def _s5_body(*refs, q, scan):
    if scan:
        u_ref, toep_ref, bin_ref, coff_ref, aqr_ref, aqi_ref, y_ref, so_ref, sr_s, si_s = refs
        s0_ref = None
    else:
        u_ref, toep_ref, bin_ref, coff_ref, aqr_ref, aqi_ref, s0_ref, y_ref, so_ref = refs
    k = S5_GROUP_SIZE
    p = S5_STATE
    gp = S5_GB * p
    w = u_ref.shape[2]
    nsq = w // LANES

    def chunk_in(gl):
        return u_ref[:, gl * k:(gl + 1) * k, :].reshape(q * k, w)

    if scan:
        for gl in range(S5_GB):
            bo = _dot(bin_ref[gl], chunk_in(gl).astype(BF16))
            sr_s[gl * p:(gl + 1) * p, :] = bo[:p, :]
            si_s[gl * p:(gl + 1) * p, :] = bo[p:, :]
        row = lax.broadcasted_iota(jnp.int32, (LANES, LANES), 0)
        lane_lo = lax.broadcasted_iota(jnp.int32, (1, LANES), 1) < p

        def shift_down(x, s):
            if s < SUBLANES:
                return jnp.where(row >= s, pltpu.roll(x, s, axis=0), 0.0)
            return jnp.concatenate([jnp.zeros((s, LANES), F32), x[:LANES - s, :]], axis=0)

        for pair in range(S5_GB // 2):
            g0 = 2 * pair
            rs = slice(g0 * p, (g0 + 2) * p)
            a_r = jnp.where(lane_lo, aqr_ref[g0], aqr_ref[g0 + 1])
            a_i = jnp.where(lane_lo, aqi_ref[g0], aqi_ref[g0 + 1])
            for sq in range(nsq):
                ls = slice(sq * LANES, (sq + 1) * LANES)
                xr = sr_s[rs, ls].T
                xi = si_s[rs, ls].T
                pr, pi = a_r, a_i
                s = 1
                while s < LANES:
                    tr, ti = shift_down(xr, s), shift_down(xi, s)
                    xr, xi = xr + (pr * tr - pi * ti), xi + (pr * ti + pi * tr)
                    if 2 * s < LANES:
                        pr, pi = pr * pr - pi * pi, 2.0 * pr * pi
                    s *= 2
                so_ref[sq, pair, 0:1, :] = xr[LANES - 1:LANES, :]
                so_ref[sq, pair, 1:2, :] = xi[LANES - 1:LANES, :]
                sr_s[rs, ls] = shift_down(xr, 1).T
                si_s[rs, ls] = shift_down(xi, 1).T

    for gl in range(S5_GB):
        z = chunk_in(gl)
        if scan:
            sin_r, sin_i = sr_s[gl * p:(gl + 1) * p, :], si_s[gl * p:(gl + 1) * p, :]
        else:
            bo = _dot_hi(bin_ref[gl], z)
            aqr, aqi = aqr_ref[gl], aqi_ref[gl]
            sin_r, sin_i = s0_ref[gl, :p, :], s0_ref[gl, p:, :]
            so_ref[gl, :p, :] = aqr * sin_r - aqi * sin_i + bo[:p, :]
            so_ref[gl, p:, :] = aqr * sin_i + aqi * sin_r + bo[p:, :]
        s_in = jnp.concatenate([sin_r, sin_i], axis=0).astype(BF16)
        y = _dot(toep_ref[gl], z.astype(BF16)) + _dot(coff_ref[gl], s_in)
        y_ref[:, gl * k:(gl + 1) * k, :] = y.reshape(q, k, w).astype(y_ref.dtype)


def _s5(ut, q, scan, prep, s0t=None):
    ctot = ut.shape[2]
    lw = min(S5_SCAN_SEQS * LANES, ctot) if scan else LANES
    nlb = ctot // lw
    ngb = S5_GROUPS // S5_GB
    qk = q * S5_GROUP_SIZE
    gspec = lambda r, c, cb=0: pl.BlockSpec((S5_GB, r, c), lambda gb, lb: (gb, 0, cb))
    in_specs = [pl.BlockSpec((q, S5_GB * S5_GROUP_SIZE, lw), lambda gb, lb: (0, gb, lb)),
                gspec(qk, qk), gspec(2 * S5_STATE, qk, (S5_Q - q) * S5_GROUP_SIZE // qk),
                gspec(qk, 2 * S5_STATE)]
    args = [ut, prep["toep"], prep["binb"] if scan else prep["bin"], prep["coff"], *prep["aq"][q]]
    scratch = []
    if scan:
        in_specs += [gspec(1, 2 * S5_STATE), gspec(1, 2 * S5_STATE)]
        so_shape = jax.ShapeDtypeStruct((ctot // LANES, S5_GROUPS // 2, 2, 2 * S5_STATE), F32)
        so_spec = pl.BlockSpec((lw // LANES, S5_GB // 2, 2, 2 * S5_STATE), lambda gb, lb: (lb, gb, 0, 0))
        scratch = [pltpu.VMEM((S5_GB * S5_STATE, lw), F32)] * 2
    else:
        in_specs += [gspec(S5_STATE, LANES), gspec(S5_STATE, LANES)]
        in_specs.append(gspec(2 * S5_STATE, LANES))
        args.append(s0t)
        so_shape = jax.ShapeDtypeStruct((S5_GROUPS, 2 * S5_STATE, LANES), F32)
        so_spec = gspec(2 * S5_STATE, LANES)
    return pl.pallas_call(
        functools.partial(_s5_body, q=q, scan=scan),
        grid=(ngb, nlb),
        in_specs=in_specs,
        out_specs=[pl.BlockSpec((q, S5_GB * S5_GROUP_SIZE, lw), lambda gb, lb: (0, gb, lb)), so_spec],
        out_shape=[jax.ShapeDtypeStruct(ut.shape, BF16), so_shape],
        scratch_shapes=scratch,
        compiler_params=_cparams(("arbitrary", "arbitrary")),
        name="s5_scan" if scan else "s5_step",
    )(*args)


OUT_ROWS = 512


def _out_body(y5_ref, u5_ref, z5_ref, ga_ref, gb_ref, ys_ref, zs_ref, x_ref, d5_ref, wglu_ref, bglu_ref,
              wd5_ref, nws_ref, wda_ref, wout_ref, fnw_ref, o_ref):
    f32 = lambda r: r[...].astype(F32)
    y5 = jax.nn.gelu(f32(y5_ref) + d5_ref[...] * f32(u5_ref))
    y5 = y5 * _sigmoid(_dot(y5.astype(BF16), wglu_ref[...]) + bglu_ref[...])
    y5 = y5 * _silu(f32(z5_ref))
    y_b = _dot(y5.astype(BF16), wd5_ref[...])
    y_a = None
    for g in range(SSD_GROUPS):
        sl = slice(g * SSD_GW, (g + 1) * SSD_GW)
        v = ys_ref[:, sl].astype(F32) * _silu(zs_ref[:, sl].astype(F32))
        ms = jnp.mean(v * v, axis=-1, keepdims=True)
        yz = (v * lax.rsqrt(ms + NORM_EPS) * nws_ref[:, sl]).astype(BF16)
        part = _dot(yz, wda_ref[sl, :])
        y_a = part if y_a is None else y_a + part
    mix = _sigmoid(f32(ga_ref)) * y_a + _sigmoid(f32(gb_ref)) * y_b
    out = x_ref[...] + _dot(mix.astype(BF16), wout_ref[...])
    ms = jnp.mean(out * out, axis=-1, keepdims=True)
    o_ref[...] = out * lax.rsqrt(ms + NORM_EPS) * fnw_ref[...]


def _out(y5, u5, proj, ys, x2d, prm):
    m = x2d.shape[0]
    rowb = lambda w, cb: pl.BlockSpec((OUT_ROWS, w), lambda i: (i, cb))
    const = lambda r, c: pl.BlockSpec((r, c), lambda i: (0, 0))
    return pl.pallas_call(
        _out_body,
        grid=(m // OUT_ROWS,),
        in_specs=[rowb(1024, 0), rowb(1024, 0), rowb(1024, CB_Z5), rowb(1024, CB_GA), rowb(1024, CB_GB),
                  rowb(SSD_WIDTH, 0), rowb(SSD_WIDTH, 0), rowb(1024, 0),
                  const(1, 1024), const(1024, 1024), const(1, 1024), const(1024, 1024),
                  const(1, SSD_WIDTH), const(SSD_WIDTH, 1024), const(1024, 1024), const(1, 1024)],
        out_specs=rowb(1024, 0),
        out_shape=jax.ShapeDtypeStruct((m, D_MODEL), F32),
        compiler_params=_cparams(("arbitrary",)),
        name="out_stage",
    )(y5, u5, proj, proj, proj, ys, proj, x2d, prm["d_s5"], prm["w_glu"], prm["b_glu"],
      prm["w_down_s5"], prm["ssd_norm_w"], prm["w_down_ssd"], prm["w_out"], prm["final_norm_w"])


def _run(x, conv0, ssm0, re0, im0, prm):
    nseq, seqlen, _ = x.shape
    m = nseq * seqlen
    prompt = conv0 is None
    x2d = x.reshape(m, D_MODEL)
    proj, dt, u5 = _in_proj(x2d, prm["norm_w"], prm["w_cat"], prm["w_dt"])
    conv_new = proj.reshape(nseq, seqlen, PROJ_COLS)[:, seqlen - (SSD_CONV - 1):,
                                                     SSD_WIDTH:SSD_WIDTH + SSD_CONV_DIM].astype(F32)

    if prompt:
        yz, ssm_new = _ssd(proj, dt, nseq, seqlen // SSD_CHUNK, prm["d_ssd"], prm["ssd_consts"])
    else:
        yz, ssm_new = _ssd_packed(proj, dt, nseq, prm["d_ssd"], prm["ssd_consts"], conv0,
                                  ssm0.reshape(nseq, SSD_WIDTH, SSD_STATE))

    q = S5_Q if prompt else seqlen
    ut = jnp.transpose(u5.reshape(m // q, q, S5_WIDTH), (1, 2, 0))
    if prompt:
        yt, so = _s5(ut, q, True, prm["s5"])
        so = so.reshape(nseq, S5_GROUPS // 2, 2, 2, S5_STATE)
        so = jnp.transpose(so, (0, 1, 3, 2, 4)).reshape(nseq, S5_GROUPS, 2, S5_STATE)
        re_new, im_new = so[:, :, 0, :], so[:, :, 1, :]
    else:
        s0t = jnp.transpose(jnp.concatenate([re0, im0], axis=-1), (1, 2, 0))
        yt, so = _s5(ut, q, False, prm["s5"], s0t)
        so = jnp.transpose(so, (2, 0, 1))
        re_new, im_new = so[:, :, :S5_STATE], so[:, :, S5_STATE:]
    y5 = jnp.transpose(yt, (2, 0, 1)).reshape(m, S5_WIDTH)

    out = _out(y5, u5, proj, yz, x2d, prm)
    return (out.reshape(nseq, seqlen, D_MODEL), conv_new[None],
            ssm_new.reshape(1, nseq, SSD_HEADS, SSD_HEAD_DIM, SSD_STATE), re_new[None], im_new[None])


def kernel(x_prompt, x_sample, state_conv, state_ssm, state_s5_re, state_s5_im, norm_w, w_in, conv_w, conv_b,
           dt_bias, A_log, D_ssd, ssd_norm_w, w_down_ssd, lam_re, lam_im, log_dt, B_re, B_im, C_re, C_im, D_s5,
           w_glu, b_glu, w_down_s5, w_out, final_norm_w):
    assert norm_w.shape[0] == 1, "single-layer trunk"
    assert x_prompt.shape[1] % SSD_CHUNK == 0 and x_prompt.shape[1] // S5_Q == LANES
    assert x_sample.shape[1] == SSD_TOK and x_sample.shape[0] == LANES
    w = w_in[0]
    o_dt = SSD_WIDTH + SSD_CONV_DIM
    w_cat = jnp.concatenate([w[:, :o_dt], w[:, o_dt + SSD_HEADS:]], axis=1).astype(BF16)
    w_dt = jnp.pad(w[:, o_dt:o_dt + SSD_HEADS], ((0, 0), (0, LANES - SSD_HEADS))).astype(BF16)
    prm = dict(norm_w=norm_w[0].reshape(1, -1), w_cat=w_cat, w_dt=w_dt,
               ssd_consts=_ssd_consts(conv_w[0], conv_b[0], dt_bias[0], A_log[0]), d_ssd=D_ssd[0],
               ssd_norm_w=ssd_norm_w[0].reshape(1, -1),
               s5=_s5_prep(lam_re[0], lam_im[0], log_dt[0], B_re[0], B_im[0], C_re[0], C_im[0]),
               d_s5=D_s5[0].reshape(1, -1), w_glu=w_glu[0].astype(BF16), b_glu=b_glu[0].reshape(1, -1),
               w_down_s5=w_down_s5[0].astype(BF16), w_down_ssd=w_down_ssd[0].astype(BF16),
               w_out=w_out[0].astype(BF16), final_norm_w=final_norm_w.reshape(1, -1))

    y_p, conv_p, ssm_p, re_p, im_p = _run(x_prompt, None, None, None, None, prm)
    y_s, conv_s, ssm_s, re_s, im_s = _run(x_sample, state_conv[0], state_ssm[0], state_s5_re[0],
                                          state_s5_im[0], prm)
    return (y_p, y_s, conv_p, ssm_p, re_p, im_p, conv_s, ssm_s, re_s, im_s)
```

```python
import functools

import jax
import jax.numpy as jnp
from jax import lax
from jax.experimental import pallas as pl
from jax.experimental.pallas import tpu as pltpu

F32 = jnp.float32
BF16 = jnp.bfloat16

D_MODEL = 1024
SSD_WIDTH = 2048
SSD_HEAD_DIM = 64
SSD_HEADS = 32
SSD_GROUPS = 8
SSD_HPG = SSD_HEADS // SSD_GROUPS
SSD_GW = SSD_WIDTH // SSD_GROUPS
SSD_STATE = 128
SSD_CONV = 4
SSD_CHUNK = 128
SSD_CONV_DIM = 4096
S5_WIDTH = 1024
S5_GROUP_SIZE = 16
S5_GROUPS = 64
S5_STATE = 64
NORM_EPS = 1e-6
LOG2E = 1.4426950408889634

LANES = 128
SUBLANES = 8

PROJ_COLS = 10240
COL_TILE = 1024
ROW_TILE = 2048
CB_B, CB_C, CB_U5, CB_Z5, CB_GA, CB_GB = 4, 5, 6, 7, 8, 9

VMEM_LIMIT = 56 * 1024 * 1024


def _cparams(sem):
    return pltpu.CompilerParams(dimension_semantics=sem, vmem_limit_bytes=VMEM_LIMIT)


def _sigmoid(x):
    return 0.5 * jnp.tanh(0.5 * x) + 0.5


def _silu(x):
    h = 0.5 * x
    return h * jnp.tanh(h) + h


def _dot(a, b):
    return jnp.dot(a, b, preferred_element_type=F32)


def _dot_nt(a, b):
    return lax.dot_general(a, b, (((1,), (1,)), ((), ())), preferred_element_type=F32)


def _dot_hi(a, b):
    a_hi = a.astype(BF16)
    a_lo = (a - a_hi.astype(F32)).astype(BF16)
    if b.dtype == BF16:
        return _dot(a_hi, b) + _dot(a_lo, b)
    b_hi = b.astype(BF16)
    b_lo = (b - b_hi.astype(F32)).astype(BF16)
    return _dot(a_hi, b_hi) + _dot(a_hi, b_lo) + _dot(a_lo, b_hi)


S5_Q = 16
PREP_GB = 8


def _zoh(lr, li, ldt):
    step = jnp.exp(ldt)
    mag = jnp.exp(lr * step)
    return mag * jnp.cos(li * step), mag * jnp.sin(li * step)


def _s5_prep_body(rowp_ref, cre_ref, cim_ref, b1_ref, b2_ref,
                  toep_ref, coff_ref, bin_ref, binb_ref, aq16r_ref, aq16i_ref, aq8r_ref, aq8i_ref, ca_s):
    k, p, q = S5_GROUP_SIZE, S5_STATE, S5_Q
    qk = q * k
    lr, li = rowp_ref[:, 0:1, :], rowp_ref[:, 1:2, :]
    ar_l, ai_l = _zoh(lr, li, rowp_ref[:, 2:3, :])
    den = lr * lr + li * li
    nr = ar_l - 1.0
    cr_l = (nr * lr + ai_l * li) / den
    ci_l = (ai_l * lr - nr * li) / den

    def on_sublanes(v):
        return jnp.stack([jnp.broadcast_to(v[g], (2 * p, 2 * p)).T for g in range(PREP_GB)])

    wide = lambda v: jnp.concatenate([v] * (qk // (2 * p)), axis=-1)

    ar, ai, cr, ci = on_sublanes(ar_l), on_sublanes(ai_l), on_sublanes(cr_l), on_sublanes(ci_l)
    sgn_row = jnp.where(lax.broadcasted_iota(jnp.int32, (PREP_GB, 2 * p, qk), 1) < p, -1.0, 1.0)
    b1, b2 = b1_ref[...], b2_ref[...]
    crw, ciw = wide(cr), sgn_row * wide(ci)
    bb = crw * b1 + ciw * b2
    bb_sw = crw * b2 - ciw * b1
    lane_t = jnp.right_shift(lax.broadcasted_iota(jnp.int32, (PREP_GB, 2 * p, qk), 2), 4)
    pr, pi = jnp.ones_like(ar), jnp.zeros_like(ar)
    pwr = jnp.zeros((PREP_GB, 2 * p, qk), F32)
    pwi = jnp.zeros((PREP_GB, 2 * p, qk), F32)
    for j in range(q):
        sel = lane_t == (q - 1 - j)
        pwr = jnp.where(sel, wide(pr), pwr)
        pwi = jnp.where(sel, wide(pi), pwi)
        if j == q // 2:
            aq8r_ref[...] = pr[:, :p, :]
            aq8i_ref[...] = pi[:, :p, :]
        pr, pi = pr * ar - pi * ai, pr * ai + pi * ar
    binv = pwr * bb + sgn_row * pwi * bb_sw
    bin_ref[...] = binv
    binb_ref[...] = binv.astype(BF16)

    ar, ai = ar_l, ai_l
    low = lax.broadcasted_iota(jnp.int32, (PREP_GB, 1, 2 * p), 2) < p
    sgn_lane = jnp.where(low, 1.0, -1.0)
    cre, cim = cre_ref[...], cim_ref[...]
    pr, pi = jnp.ones_like(ar), jnp.zeros_like(ar)
    for j in range(q + 1):
        p1 = jnp.where(low, pr, pi)
        p2 = jnp.where(low, pi, pr)
        ca_s[:, j * k:(j + 1) * k, :] = sgn_lane * cre * p1 - cim * p2
        if j == q:
            aq16r_ref[...] = pr
            aq16i_ref[...] = pi
        pr, pi = pr * ar - pi * ai, pr * ai + pi * ar
    coff_ref[...] = ca_s[:, k:, :].astype(BF16)

    lane_blk = jnp.right_shift(lax.broadcasted_iota(jnp.int32, (qk, qk), 1), 4)
    for g in range(PREP_GB):
        km = _dot_hi(ca_s[g, :qk, :], bb[g])
        tp = jnp.where(lane_blk == 0, km, 0.0)
        for t in range(1, q):
            sh = jnp.concatenate([jnp.zeros((t * k, qk), F32), km[:qk - t * k, :]], axis=0)
            tp = jnp.where(lane_blk == t, sh, tp)
        toep_ref[g] = tp.astype(BF16)


def _s5_prep(lam_re, lam_im, log_dt, b_re, b_im, c_re, c_im):
    g, p, k, q = S5_GROUPS, S5_STATE, S5_GROUP_SIZE, S5_Q
    qk = q * k
    dup = lambda a: jnp.concatenate([a, a], axis=-1)
    ldt = jnp.broadcast_to(log_dt[:, None], (g, 2 * p))
    rowp = jnp.stack([dup(lam_re), dup(lam_im), ldt], axis=1)
    b1 = jnp.tile(jnp.concatenate([b_re, b_im], axis=1), (1, 1, q))
    b2 = jnp.tile(jnp.concatenate([b_im, b_re], axis=1), (1, 1, q))
    blk = lambda r, c: pl.BlockSpec((PREP_GB, r, c), lambda i: (i, 0, 0))
    shp = lambda r, c, dt: jax.ShapeDtypeStruct((g, r, c), dt)
    toep, coff, bin_, binb, aq16r, aq16i, aq8r, aq8i = pl.pallas_call(
        _s5_prep_body,
        grid=(g // PREP_GB,),
        in_specs=[blk(3, 2 * p), blk(k, 2 * p), blk(k, 2 * p), blk(2 * p, qk), blk(2 * p, qk)],
        out_specs=[blk(qk, qk), blk(qk, 2 * p), blk(2 * p, qk), blk(2 * p, qk),
                   blk(1, 2 * p), blk(1, 2 * p), blk(p, LANES), blk(p, LANES)],
        out_shape=[shp(qk, qk, BF16), shp(qk, 2 * p, BF16), shp(2 * p, qk, F32), shp(2 * p, qk, BF16),
                   shp(1, 2 * p, F32), shp(1, 2 * p, F32), shp(p, LANES, F32), shp(p, LANES, F32)],
        scratch_shapes=[pltpu.VMEM((PREP_GB, (q + 1) * k, 2 * p), F32)],
        compiler_params=_cparams(("arbitrary",)),
        name="s5_prep",
    )(rowp, dup(c_re), dup(c_im), b1, b2)
    return dict(toep=toep, coff=coff, bin=bin_, binb=binb,
                aq={16: (aq16r, aq16i), 8: (aq8r, aq8i)})


def _in_proj_body(x_ref, nw_ref, w_ref, wdt_ref, o_ref, dt_ref, u5_ref, h_ref):
    j = pl.program_id(1)

    @pl.when(j == 0)
    def _():
        x = x_ref[...]
        ms = jnp.mean(x * x, axis=-1, keepdims=True)
        h = (x * lax.rsqrt(ms + NORM_EPS) * nw_ref[...]).astype(BF16)
        h_ref[...] = h
        dt_ref[...] = _dot(h, wdt_ref[...])

    o = _dot(h_ref[...], w_ref[...]).astype(BF16)
    o_ref[...] = o

    @pl.when(j == CB_U5)
    def _():
        u5_ref[...] = o


def _in_proj(x2d, norm_w, w_cat, w_dt):
    m = x2d.shape[0]
    rt = min(ROW_TILE, m)
    return pl.pallas_call(
        _in_proj_body,
        grid=(m // rt, PROJ_COLS // COL_TILE),
        in_specs=[pl.BlockSpec((rt, D_MODEL), lambda i, j: (i, 0)),
                  pl.BlockSpec((1, D_MODEL), lambda i, j: (0, 0)),
                  pl.BlockSpec((D_MODEL, COL_TILE), lambda i, j: (0, j)),
                  pl.BlockSpec((D_MODEL, LANES), lambda i, j: (0, 0))],
        out_specs=[pl.BlockSpec((rt, COL_TILE), lambda i, j: (i, j)),
                   pl.BlockSpec((rt, LANES), lambda i, j: (i, 0)),
                   pl.BlockSpec((rt, COL_TILE), lambda i, j: (i, 0))],
        out_shape=[jax.ShapeDtypeStruct((m, PROJ_COLS), BF16), jax.ShapeDtypeStruct((m, LANES), F32),
                   jax.ShapeDtypeStruct((m, S5_WIDTH), BF16)],
        scratch_shapes=[pltpu.VMEM((rt, D_MODEL), BF16)],
        compiler_params=_cparams(("arbitrary", "arbitrary")),
        name="in_proj",
    )(x2d, norm_w, w_cat, w_dt)


CONV_PAD = SUBLANES
CONV_SEC = 256


def _seg_cumsum_rows(x, seg):
    t = lax.broadcasted_iota(jnp.int32, x.shape, 0) & (seg - 1)
    s = 1
    while s < seg:
        x = x + jnp.where(t >= s, pltpu.roll(x, s, axis=0), 0.0)
        s *= 2
    return x


def _seg_revsum_rows(x, seg):
    n = x.shape[0]
    t = lax.broadcasted_iota(jnp.int32, x.shape, 0) & (seg - 1)
    s = 1
    while s < seg:
        x = x + jnp.where(t + s < seg, pltpu.roll(x, n - s, axis=0), 0.0)
        s *= 2
    return x


def _to_channel_major(src, dst):
    for j in range(SSD_WIDTH // LANES):
        sl = slice(j * LANES, (j + 1) * LANES)
        dst[sl, :] = src[:, sl].astype(F32).T


def _store_time_major(yt, y_ref):
    for j in range(SSD_WIDTH // LANES):
        sl = slice(j * LANES, (j + 1) * LANES)
        y_ref[:, sl] = yt[sl, :].T.astype(y_ref.dtype)


def _group_bc(act, g):
    bg = act[:, SSD_WIDTH + g * SSD_STATE:SSD_WIDTH + (g + 1) * SSD_STATE].astype(BF16)
    cg = act[:, SSD_WIDTH + 1024 + g * SSD_STATE:SSD_WIDTH + 1024 + (g + 1) * SSD_STATE].astype(BF16)
    return bg, cg


SSD_CPS = 4


def _ssd_body(d_ref, xs_ref, b_ref, c_ref, dt_ref, cw_ref, cb_ref, dtb_ref, alog_ref,
              y_ref, ssmo_ref, cbuf, act, xt, yt, state):
    q = SSD_CHUNK
    hp = SSD_HEAD_DIM

    k1 = SSD_CONV - 1
    base = CONV_PAD

    @pl.when(pl.program_id(1) == 0)
    def _():
        state[...] = jnp.zeros_like(state)
        cbuf[:base, :] = jnp.zeros((base, SSD_CONV_DIM), F32)

    causal = lax.broadcasted_iota(jnp.int32, (q, q), 1) >= lax.broadcasted_iota(jnp.int32, (q, q), 0)

    def chunk(ci, carry):
        rows = pl.ds(pl.multiple_of(ci * q, q), q)
        cbuf[base:, 0:SSD_WIDTH] = xs_ref[rows, :].astype(F32)
        cbuf[base:, SSD_WIDTH:SSD_WIDTH + 1024] = b_ref[rows, :].astype(F32)
        cbuf[base:, SSD_WIDTH + 1024:] = c_ref[rows, :].astype(F32)
        for s0 in range(0, SSD_CONV_DIM, CONV_SEC):
            cs = slice(s0, s0 + CONV_SEC)
            full = cbuf[:, cs]
            acc = cb_ref[:, cs] + cw_ref[k1:k1 + 1, cs] * full[base:, :]
            for d in range(1, SSD_CONV):
                acc = acc + cw_ref[k1 - d:k1 - d + 1, cs] * pltpu.roll(full, d, axis=0)[base:, :]
            act[:, cs] = _silu(acc)
        cbuf[base - k1:base, :] = cbuf[base + q - k1:base + q, :]

        dt = jax.nn.softplus(dt_ref[rows, :] + dtb_ref[...])
        a_cs = _seg_cumsum_rows(dt * (-jnp.exp(alog_ref[...])), q) * LOG2E
        a_cst = a_cs.T
        dtt = dt.T
        _to_channel_major(act, xt)

        for g in range(SSD_GROUPS):
            bg, cg = _group_bc(act, g)
            cbt = _dot_nt(bg, cg)
            stg = state[g * SSD_GW:(g + 1) * SSD_GW, :]
            yoff = _dot_nt(stg.astype(BF16), cg)
            for r in range(SSD_HPG):
                h = g * SSD_HPG + r
                hs = slice(h * hp, (h + 1) * hp)
                arow = a_cst[h:h + 1, :]
                acol = a_cs[:, h:h + 1]
                lt = jnp.where(causal, jnp.exp2(arow - acol), 0.0)
                wt = (cbt * lt).astype(BF16)
                xh = xt[hs, :]
                xdt = xh * dtt[h:h + 1, :]
                yt[hs, :] = (_dot(xdt.astype(BF16), wt) + yoff[r * hp:(r + 1) * hp, :] * jnp.exp2(arow)
                             + d_ref[h] * xh)
                atot = arow[:, q - 1:q]
                xdec = xdt * jnp.exp2(atot - arow)
                state[hs, :] = jnp.exp2(atot) * stg[r * hp:(r + 1) * hp, :] + _dot(xdec.astype(BF16), bg)

        _store_time_major(yt, y_ref.at[rows, :])
        return carry

    lax.fori_loop(0, SSD_CPS, chunk, 0)
    ssmo_ref[...] = state[...]


def _ssd_consts(conv_w, conv_b, dt_bias, a_log):
    padl = lambda v: jnp.pad(v, (0, LANES - SSD_HEADS)).reshape(1, LANES)
    return (conv_w, conv_b.reshape(1, -1), padl(dt_bias), padl(a_log))


def _ssd_const_specs():
    const = lambda shape: pl.BlockSpec(shape, lambda *_: (0, 0))
    return [const((SSD_CONV, SSD_CONV_DIM)), const((1, SSD_CONV_DIM)), const((1, LANES)), const((1, LANES))]


def _ssd_proj_specs(rows, rowf):
    return [pl.BlockSpec((rows, SSD_WIDTH), lambda *a: (rowf(*a), 1)),
            pl.BlockSpec((rows, 1024), lambda *a: (rowf(*a), CB_B)),
            pl.BlockSpec((rows, 1024), lambda *a: (rowf(*a), CB_C)),
            pl.BlockSpec((rows, LANES), lambda *a: (rowf(*a), 0))]


def _ssd(proj, dt, nseq, nchunk, d_ssd, consts):
    m = proj.shape[0]
    nstep = nchunk // SSD_CPS
    rows = SSD_CPS * SSD_CHUNK
    rowf = lambda b, c: b * nstep + c
    chan = pltpu.VMEM((SSD_WIDTH, LANES), F32)
    return pl.pallas_call(
        _ssd_body,
        grid=(nseq, nstep),
        in_specs=[pl.BlockSpec(memory_space=pltpu.SMEM)] + _ssd_proj_specs(rows, rowf) + _ssd_const_specs(),
        out_specs=[pl.BlockSpec((rows, SSD_WIDTH), lambda b, c: (rowf(b, c), 0)),
                   pl.BlockSpec((None, SSD_WIDTH, SSD_STATE), lambda b, c: (b, 0, 0))],
        out_shape=[jax.ShapeDtypeStruct((m, SSD_WIDTH), BF16),
                   jax.ShapeDtypeStruct((nseq, SSD_WIDTH, SSD_STATE), F32)],
        scratch_shapes=[pltpu.VMEM((CONV_PAD + SSD_CHUNK, SSD_CONV_DIM), F32),
                        pltpu.VMEM((SSD_CHUNK, SSD_CONV_DIM), F32),
                        chan, chan, pltpu.VMEM((SSD_WIDTH, SSD_STATE), F32)],
        compiler_params=_cparams(("arbitrary", "arbitrary")),
        name="ssd",
    )(d_ssd, proj, proj, proj, dt, *consts)


SSD_PACK = 16
SSD_TOK = SSD_CHUNK // SSD_PACK
SSD_SUB = 8


def _ssd_decay_body(dt_ref, dtb_ref, alog_ref, o_ref):
    da = jax.nn.softplus(dt_ref[...] + dtb_ref[...]) * (-jnp.exp(alog_ref[...]))
    o_ref[...] = jnp.exp(jnp.sum(da.reshape(o_ref.shape[0], SSD_TOK, LANES), axis=1))


def _ssd_decay(dt, nseq, dtb, alog):
    const = lambda shape: pl.BlockSpec(shape, lambda i: (0, 0))
    return pl.pallas_call(
        _ssd_decay_body,
        grid=(1,),
        in_specs=[const((nseq * SSD_TOK, LANES)), const((1, LANES)), const((1, LANES))],
        out_specs=const((nseq, LANES)),
        out_shape=jax.ShapeDtypeStruct((nseq, LANES), F32),
        compiler_params=_cparams(("arbitrary",)),
        name="ssd_decay",
    )(dt, dtb, alog)


def _ssd_packed_body(dec_ref, d_ref, xs_ref, b_ref, c_ref, dt_ref, cw_ref, cb_ref, dtb_ref, alog_ref,
                     convp_ref, ssm0_ref,
                     y_ref, ssmo_ref, act, xt, yt, xd, ear, bgs, cgs):
    q = SSD_CHUNK
    tl = SSD_TOK
    hp = SSD_HEAD_DIM
    tile = pl.program_id(0)
    sub = pl.program_id(1)

    @pl.when(sub == 0)
    def _shared():
        rowt = lax.broadcasted_iota(jnp.int32, (q, CONV_SEC), 0) & (tl - 1)
        for s0 in range(0, SSD_CONV_DIM, CONV_SEC):
            cs = slice(s0, s0 + CONV_SEC)
            if s0 < SSD_WIDTH:
                x = xs_ref[:, cs]
            elif s0 < SSD_WIDTH + 1024:
                x = b_ref[:, s0 - SSD_WIDTH:s0 - SSD_WIDTH + CONV_SEC]
            else:
                x = c_ref[:, s0 - SSD_WIDTH - 1024:s0 - SSD_WIDTH - 1024 + CONV_SEC]
            x = x.astype(F32)
            e = convp_ref[:, :, cs].reshape(q, CONV_SEC)
            acc = cb_ref[:, cs] + cw_ref[SSD_CONV - 1:SSD_CONV, cs] * x
            for d in range(1, SSD_CONV):
                prev = jnp.where(rowt >= d, pltpu.roll(x, d, axis=0), pltpu.roll(e, q - tl + d, axis=0))
                acc = acc + cw_ref[SSD_CONV - 1 - d:SSD_CONV - d, cs] * prev
            act[:, cs] = _silu(acc)

        dt = jax.nn.softplus(dt_ref[...] + dtb_ref[...])
        da = dt * (-jnp.exp(alog_ref[...]))
        a_cs = _seg_cumsum_rows(da, tl)
        a_tot = a_cs + _seg_revsum_rows(da, tl) - da
        a_cst, a_tott, dtt = a_cs.T, a_tot.T, dt.T
        ea = jnp.exp(a_cst)
        _to_channel_major(act, xt)

        s_i = lax.broadcasted_iota(jnp.int32, (q, q), 0)
        l_i = lax.broadcasted_iota(jnp.int32, (q, q), 1)
        causal = (l_i >= s_i) & (jnp.right_shift(l_i, 3) == jnp.right_shift(s_i, 3))
        for g in range(SSD_GROUPS):
            bg, cg = _group_bc(act, g)
            bgs[g] = bg
            cgs[g] = cg
            cbt = _dot_nt(bg, cg)
            for r in range(SSD_HPG):
                h = g * SSD_HPG + r
                hs = slice(h * hp, (h + 1) * hp)
                arow = a_cst[h:h + 1, :]
                acol = a_cs[:, h:h + 1]
                lt = jnp.where(causal, jnp.exp(arow - acol), 0.0)
                wt = (cbt * lt).astype(BF16)
                xh = xt[hs, :]
                xdt = xh * dtt[h:h + 1, :]
                yt[hs, :] = _dot(xdt.astype(BF16), wt) + d_ref[h] * xh
                xd[hs, :] = xdt * jnp.exp(a_tott[h:h + 1, :] - arow)
                ear[hs, :] = jnp.broadcast_to(ea[h:h + 1, :], (hp, q))

    lane_seq = jnp.right_shift(lax.broadcasted_iota(jnp.int32, (SSD_GW, q), 1), 3)

    def seq_body(sl, carry):
        sq = sub * SSD_SUB + sl
        inseq = lane_seq == sq
        dbase = (tile * SSD_PACK + sq) * SSD_HEADS
        for g in range(SSD_GROUPS):
            rows = slice(g * SSD_GW, (g + 1) * SSD_GW)
            stg = ssm0_ref[sl, rows, :]
            yo = _dot_nt(stg.astype(BF16), cgs[g])
            yt[rows, :] = yt[rows, :] + jnp.where(inseq, yo * ear[rows, :], 0.0)
            upd = _dot(jnp.where(inseq, xd[rows, :], 0.0).astype(BF16), bgs[g])
            for r in range(SSD_HPG):
                h = g * SSD_HPG + r
                ssmo_ref[sl, h * hp:(h + 1) * hp, :] = (dec_ref[dbase + h] * stg[r * hp:(r + 1) * hp, :]
                                                        + upd[r * hp:(r + 1) * hp, :])
        return carry

    lax.fori_loop(0, SSD_SUB, seq_body, 0)

    @pl.when(sub == SSD_PACK // SSD_SUB - 1)
    def _():
        _store_time_major(yt, y_ref)


def _ssd_packed(proj, dt, nseq, d_ssd, consts, conv0, ssm0):
    m = proj.shape[0]
    ntile = nseq // SSD_PACK
    nsub = SSD_PACK // SSD_SUB
    dec = _ssd_decay(dt, nseq, consts[2], consts[3])[:, :SSD_HEADS].reshape(-1)
    convp = jnp.pad(conv0, ((0, 0), (SSD_TOK - (SSD_CONV - 1), 0), (0, 0)))
    chan = pltpu.VMEM((SSD_WIDTH, LANES), F32)
    st_spec = pl.BlockSpec((SSD_SUB, SSD_WIDTH, SSD_STATE), lambda i, s: (i * nsub + s, 0, 0))
    return pl.pallas_call(
        _ssd_packed_body,
        grid=(ntile, nsub),
        in_specs=[pl.BlockSpec(memory_space=pltpu.SMEM), pl.BlockSpec(memory_space=pltpu.SMEM)]
        + _ssd_proj_specs(SSD_CHUNK, lambda i, s: i) + _ssd_const_specs()
        + [pl.BlockSpec((SSD_PACK, SSD_TOK, SSD_CONV_DIM), lambda i, s: (i, 0, 0)), st_spec],
        out_specs=[pl.BlockSpec((SSD_CHUNK, SSD_WIDTH), lambda i, s: (i, 0)), st_spec],
        out_shape=[jax.ShapeDtypeStruct((m, SSD_WIDTH), BF16),
                   jax.ShapeDtypeStruct((nseq, SSD_WIDTH, SSD_STATE), F32)],
        scratch_shapes=[pltpu.VMEM((SSD_CHUNK, SSD_CONV_DIM), F32), chan, chan, chan, chan,
                        pltpu.VMEM((SSD_GROUPS, SSD_CHUNK, SSD_STATE), BF16),
                        pltpu.VMEM((SSD_GROUPS, SSD_CHUNK, SSD_STATE), BF16)],
        compiler_params=_cparams(("arbitrary", "arbitrary")),
        name="ssd_packed",
    )(dec, d_ssd, proj, proj, proj, dt, *consts, convp, ssm0)


S5_GB = 16
S5_SCAN_SEQS = 2


def _s5_body(*refs, q, scan):
    if scan:
        u_ref, toep_ref, bin_ref, coff_ref, aqr_ref, aqi_ref, y_ref, so_ref, sr_s, si_s = refs
        s0_ref = None
    else:
        u_ref, toep_ref, bin_ref, coff_ref, aqr_ref, aqi_ref, s0_ref, y_ref, so_ref = refs
    k = S5_GROUP_SIZE
    p = S5_STATE
    gp = S5_GB * p
    w = u_ref.shape[2]
    nsq = w // LANES

    def chunk_in(gl):
        return u_ref[:, gl * k:(gl + 1) * k, :].reshape(q * k, w)

    if scan:
        for gl in range(S5_GB):
            bo = _dot(bin_ref[gl], chunk_in(gl).astype(BF16))
            sr_s[gl * p:(gl + 1) * p, :] = bo[:p, :]
            si_s[gl * p:(gl + 1) * p, :] = bo[p:, :]
        row = lax.broadcasted_iota(jnp.int32, (LANES, LANES), 0)
        lane_lo = lax.broadcasted_iota(jnp.int32, (1, LANES), 1) < p

        def shift_down(x, s):
            if s < SUBLANES:
                return jnp.where(row >= s, pltpu.roll(x, s, axis=0), 0.0)
            return jnp.concatenate([jnp.zeros((s, LANES), F32), x[:LANES - s, :]], axis=0)

        for pair in range(S5_GB // 2):
            g0 = 2 * pair
            rs = slice(g0 * p, (g0 + 2) * p)
            a_r = jnp.where(lane_lo, aqr_ref[g0], aqr_ref[g0 + 1])
            a_i = jnp.where(lane_lo, aqi_ref[g0], aqi_ref[g0 + 1])
            for sq in range(nsq):
                ls = slice(sq * LANES, (sq + 1) * LANES)
                xr = sr_s[rs, ls].T
                xi = si_s[rs, ls].T
                pr, pi = a_r, a_i
                s = 1
                while s < LANES:
                    tr, ti = shift_down(xr, s), shift_down(xi, s)
                    xr, xi = xr + (pr * tr - pi * ti), xi + (pr * ti + pi * tr)
                    if 2 * s < LANES:
                        pr, pi = pr * pr - pi * pi, 2.0 * pr * pi
                    s *= 2
                so_ref[sq, pair, 0:1, :] = xr[LANES - 1:LANES, :]
                so_ref[sq, pair, 1:2, :] = xi[LANES - 1:LANES, :]
                sr_s[rs, ls] = shift_down(xr, 1).T
                si_s[rs, ls] = shift_down(xi, 1).T

    for gl in range(S5_GB):
        z = chunk_in(gl)
        if scan:
            sin_r, sin_i = sr_s[gl * p:(gl + 1) * p, :], si_s[gl * p:(gl + 1) * p, :]
        else:
            bo = _dot_hi(bin_ref[gl], z)
            aqr, aqi = aqr_ref[gl], aqi_ref[gl]
            sin_r, sin_i = s0_ref[gl, :p, :], s0_ref[gl, p:, :]
            so_ref[gl, :p, :] = aqr * sin_r - aqi * sin_i + bo[:p, :]
            so_ref[gl, p:, :] = aqr * sin_i + aqi * sin_r + bo[p:, :]
        s_in = jnp.concatenate([sin_r, sin_i], axis=0).astype(BF16)
        y = _dot(toep_ref[gl], z.astype(BF16)) + _dot(coff_ref[gl], s_in)
        y_ref[:, gl * k:(gl + 1) * k, :] = y.reshape(q, k, w).astype(y_ref.dtype)


def _s5(ut, q, scan, prep, s0t=None):
    ctot = ut.shape[2]
    lw = min(S5_SCAN_SEQS * LANES, ctot) if scan else LANES
    nlb = ctot // lw
    ngb = S5_GROUPS // S5_GB
    qk = q * S5_GROUP_SIZE
    gspec = lambda r, c, cb=0: pl.BlockSpec((S5_GB, r, c), lambda gb, lb: (gb, 0, cb))
    in_specs = [pl.BlockSpec((q, S5_GB * S5_GROUP_SIZE, lw), lambda gb, lb: (0, gb, lb)),
                gspec(qk, qk), gspec(2 * S5_STATE, qk, (S5_Q - q) * S5_GROUP_SIZE // qk),
                gspec(qk, 2 * S5_STATE)]
    args = [ut, prep["toep"], prep["binb"] if scan else prep["bin"], prep["coff"], *prep["aq"][q]]
    scratch = []
    if scan:
        in_specs += [gspec(1, 2 * S5_STATE), gspec(1, 2 * S5_STATE)]
        so_shape = jax.ShapeDtypeStruct((ctot // LANES, S5_GROUPS // 2, 2, 2 * S5_STATE), F32)
        so_spec = pl.BlockSpec((lw // LANES, S5_GB // 2, 2, 2 * S5_STATE), lambda gb, lb: (lb, gb, 0, 0))
        scratch = [pltpu.VMEM((S5_GB * S5_STATE, lw), F32)] * 2
    else:
        in_specs += [gspec(S5_STATE, LANES), gspec(S5_STATE, LANES)]
        in_specs.append(gspec(2 * S5_STATE, LANES))
        args.append(s0t)
        so_shape = jax.ShapeDtypeStruct((S5_GROUPS, 2 * S5_STATE, LANES), F32)
        so_spec = gspec(2 * S5_STATE, LANES)
    return pl.pallas_call(
        functools.partial(_s5_body, q=q, scan=scan),
        grid=(ngb, nlb),
        in_specs=in_specs,
        out_specs=[pl.BlockSpec((q, S5_GB * S5_GROUP_SIZE, lw), lambda gb, lb: (0, gb, lb)), so_spec],
        out_shape=[jax.ShapeDtypeStruct(ut.shape, BF16), so_shape],
        scratch_shapes=scratch,
        compiler_params=_cparams(("arbitrary", "arbitrary")),
        name="s5_scan" if scan else "s5_step",
    )(*args)


OUT_ROWS = 512


def _out_body(y5_ref, u5_ref, z5_ref, ga_ref, gb_ref, ys_ref, zs_ref, x_ref, d5_ref, wglu_ref, bglu_ref,
              wd5_ref, nws_ref, wda_ref, wout_ref, fnw_ref, o_ref):
    f32 = lambda r: r[...].astype(F32)
    y5 = jax.nn.gelu(f32(y5_ref) + d5_ref[...] * f32(u5_ref))
    y5 = y5 * _sigmoid(_dot(y5.astype(BF16), wglu_ref[...]) + bglu_ref[...])
    y5 = y5 * _silu(f32(z5_ref))
    y_b = _dot(y5.astype(BF16), wd5_ref[...])
    y_a = None
    for g in range(SSD_GROUPS):
        sl = slice(g * SSD_GW, (g + 1) * SSD_GW)
        v = ys_ref[:, sl].astype(F32) * _silu(zs_ref[:, sl].astype(F32))
        ms = jnp.mean(v * v, axis=-1, keepdims=True)
        yz = (v * lax.rsqrt(ms + NORM_EPS) * nws_ref[:, sl]).astype(BF16)
        part = _dot(yz, wda_ref[sl, :])
        y_a = part if y_a is None else y_a + part
    mix = _sigmoid(f32(ga_ref)) * y_a + _sigmoid(f32(gb_ref)) * y_b
    out = x_ref[...] + _dot(mix.astype(BF16), wout_ref[...])
    ms = jnp.mean(out * out, axis=-1, keepdims=True)
    o_ref[...] = out * lax.rsqrt(ms + NORM_EPS) * fnw_ref[...]


def _out(y5, u5, proj, ys, x2d, prm):
    m = x2d.shape[0]
    rowb = lambda w, cb: pl.BlockSpec((OUT_ROWS, w), lambda i: (i, cb))
    const = lambda r, c: pl.BlockSpec((r, c), lambda i: (0, 0))
    return pl.pallas_call(
        _out_body,
        grid=(m // OUT_ROWS,),
        in_specs=[rowb(1024, 0), rowb(1024, 0), rowb(1024, CB_Z5), rowb(1024, CB_GA), rowb(1024, CB_GB),
                  rowb(SSD_WIDTH, 0), rowb(SSD_WIDTH, 0), rowb(1024, 0),
                  const(1, 1024), const(1024, 1024), const(1, 1024), const(1024, 1024),
                  const(1, SSD_WIDTH), const(SSD_WIDTH, 1024), const(1024, 1024), const(1, 1024)],
        out_specs=rowb(1024, 0),
        out_shape=jax.ShapeDtypeStruct((m, D_MODEL), F32),
        compiler_params=_cparams(("arbitrary",)),
        name="out_stage",
    )(y5, u5, proj, proj, proj, ys, proj, x2d, prm["d_s5"], prm["w_glu"], prm["b_glu"],
      prm["w_down_s5"], prm["ssd_norm_w"], prm["w_down_ssd"], prm["w_out"], prm["final_norm_w"])


def _run(x, conv0, ssm0, re0, im0, prm):
    nseq, seqlen, _ = x.shape
    m = nseq * seqlen
    prompt = conv0 is None
    x2d = x.reshape(m, D_MODEL)
    proj, dt, u5 = _in_proj(x2d, prm["norm_w"], prm["w_cat"], prm["w_dt"])
    conv_new = proj.reshape(nseq, seqlen, PROJ_COLS)[:, seqlen - (SSD_CONV - 1):,
                                                     SSD_WIDTH:SSD_WIDTH + SSD_CONV_DIM].astype(F32)

    if prompt:
        yz, ssm_new = _ssd(proj, dt, nseq, seqlen // SSD_CHUNK, prm["d_ssd"], prm["ssd_consts"])
    else:
        yz, ssm_new = _ssd_packed(proj, dt, nseq, prm["d_ssd"], prm["ssd_consts"], conv0,
                                  ssm0.reshape(nseq, SSD_WIDTH, SSD_STATE))

    q = S5_Q if prompt else seqlen
    ut = jnp.transpose(u5.reshape(m // q, q, S5_WIDTH), (1, 2, 0))
    if prompt:
        yt, so = _s5(ut, q, True, prm["s5"])
        so = so.reshape(nseq, S5_GROUPS // 2, 2, 2, S5_STATE)
        so = jnp.transpose(so, (0, 1, 3, 2, 4)).reshape(nseq, S5_GROUPS, 2, S5_STATE)
        re_new, im_new = so[:, :, 0, :], so[:, :, 1, :]
    else:
        s0t = jnp.transpose(jnp.concatenate([re0, im0], axis=-1), (1, 2, 0))
        yt, so = _s5(ut, q, False, prm["s5"], s0t)
        so = jnp.transpose(so, (2, 0, 1))
        re_new, im_new = so[:, :, :S5_STATE], so[:, :, S5_STATE:]
    y5 = jnp.transpose(yt, (2, 0, 1)).reshape(m, S5_WIDTH)

    out = _out(y5, u5, proj, yz, x2d, prm)
    return (out.reshape(nseq, seqlen, D_MODEL), conv_new[None],
            ssm_new.reshape(1, nseq, SSD_HEADS, SSD_HEAD_DIM, SSD_STATE), re_new[None], im_new[None])


def kernel(x_prompt, x_sample, state_conv, state_ssm, state_s5_re, state_s5_im, norm_w, w_in, conv_w, conv_b,
           dt_bias, A_log, D_ssd, ssd_norm_w, w_down_ssd, lam_re, lam_im, log_dt, B_re, B_im, C_re, C_im, D_s5,
           w_glu, b_glu, w_down_s5, w_out, final_norm_w):
    assert norm_w.shape[0] == 1, "single-layer trunk"
    assert x_prompt.shape[1] % SSD_CHUNK == 0 and x_prompt.shape[1] // S5_Q == LANES
    assert x_sample.shape[1] == SSD_TOK and x_sample.shape[0] == LANES
    w = w_in[0]
    o_dt = SSD_WIDTH + SSD_CONV_DIM
    w_cat = jnp.concatenate([w[:, :o_dt], w[:, o_dt + SSD_HEADS:]], axis=1).astype(BF16)
    w_dt = jnp.pad(w[:, o_dt:o_dt + SSD_HEADS], ((0, 0), (0, LANES - SSD_HEADS))).astype(BF16)
    prm = dict(norm_w=norm_w[0].reshape(1, -1), w_cat=w_cat, w_dt=w_dt,
               ssd_consts=_ssd_consts(conv_w[0], conv_b[0], dt_bias[0], A_log[0]), d_ssd=D_ssd[0],
               ssd_norm_w=ssd_norm_w[0].reshape(1, -1),
               s5=_s5_prep(lam_re[0], lam_im[0], log_dt[0], B_re[0], B_im[0], C_re[0], C_im[0]),
               d_s5=D_s5[0].reshape(1, -1), w_glu=w_glu[0].astype(BF16), b_glu=b_glu[0].reshape(1, -1),
               w_down_s5=w_down_s5[0].astype(BF16), w_down_ssd=w_down_ssd[0].astype(BF16),
               w_out=w_out[0].astype(BF16), final_norm_w=final_norm_w.reshape(1, -1))

    y_p, conv_p, ssm_p, re_p, im_p = _run(x_prompt, None, None, None, None, prm)
    y_s, conv_s, ssm_s, re_s, im_s = _run(x_sample, state_conv[0], state_ssm[0], state_s5_re[0],
                                          state_s5_im[0], prm)
    return (y_p, y_s, conv_p, ssm_p, re_p, im_p, conv_s, ssm_s, re_s, im_s)
```

```python
import functools

import jax
import jax.numpy as jnp
from jax import lax
from jax.experimental import pallas as pl
from jax.experimental.pallas import tpu as pltpu

F32 = jnp.float32
BF16 = jnp.bfloat16

D_MODEL = 1024
SSD_WIDTH = 2048
SSD_HEAD_DIM = 64
SSD_HEADS = 32
SSD_GROUPS = 8
SSD_HPG = SSD_HEADS // SSD_GROUPS
SSD_GW = SSD_WIDTH // SSD_GROUPS
SSD_STATE = 128
SSD_CONV = 4
SSD_CHUNK = 128
SSD_CONV_DIM = 4096
S5_WIDTH = 1024
S5_GROUP_SIZE = 16
S5_GROUPS = 64
S5_STATE = 64
NORM_EPS = 1e-6
LOG2E = 1.4426950408889634

LANES = 128
SUBLANES = 8

PROJ_COLS = 10240
COL_TILE = 1024
ROW_TILE = 2048
CB_B, CB_C, CB_U5, CB_Z5, CB_GA, CB_GB = 4, 5, 6, 7, 8, 9

VMEM_LIMIT = 56 * 1024 * 1024


def _cparams(sem):
    return pltpu.CompilerParams(dimension_semantics=sem, vmem_limit_bytes=VMEM_LIMIT)


def _sigmoid(x):
    return 0.5 * jnp.tanh(0.5 * x) + 0.5


def _silu(x):
    h = 0.5 * x
    return h * jnp.tanh(h) + h


def _dot(a, b):
    return jnp.dot(a, b, preferred_element_type=F32)


def _dot_nt(a, b):
    return lax.dot_general(a, b, (((1,), (1,)), ((), ())), preferred_element_type=F32)


def _dot_hi(a, b):
    a_hi = a.astype(BF16)
    a_lo = (a - a_hi.astype(F32)).astype(BF16)
    if b.dtype == BF16:
        return _dot(a_hi, b) + _dot(a_lo, b)
    b_hi = b.astype(BF16)
    b_lo = (b - b_hi.astype(F32)).astype(BF16)
    return _dot(a_hi, b_hi) + _dot(a_hi, b_lo) + _dot(a_lo, b_hi)


S5_Q = 16
PREP_GB = 8


def _zoh(lr, li, ldt):
    step = jnp.exp(ldt)
    mag = jnp.exp(lr * step)
    return mag * jnp.cos(li * step), mag * jnp.sin(li * step)


def _s5_prep_body(rowp_ref, cre_ref, cim_ref, b1_ref, b2_ref,
                  toep_ref, coff_ref, bin_ref, binb_ref, aq16r_ref, aq16i_ref, aq8r_ref, aq8i_ref, ca_s):
    k, p, q = S5_GROUP_SIZE, S5_STATE, S5_Q
    qk = q * k
    lr, li = rowp_ref[:, 0:1, :], rowp_ref[:, 1:2, :]
    ar_l, ai_l = _zoh(lr, li, rowp_ref[:, 2:3, :])
    den = lr * lr + li * li
    nr = ar_l - 1.0
    cr_l = (nr * lr + ai_l * li) / den
    ci_l = (ai_l * lr - nr * li) / den

    def on_sublanes(v):
        return jnp.stack([jnp.broadcast_to(v[g], (2 * p, 2 * p)).T for g in range(PREP_GB)])

    wide = lambda v: jnp.concatenate([v] * (qk // (2 * p)), axis=-1)

    ar, ai, cr, ci = on_sublanes(ar_l), on_sublanes(ai_l), on_sublanes(cr_l), on_sublanes(ci_l)
    sgn_row = jnp.where(lax.broadcasted_iota(jnp.int32, (PREP_GB, 2 * p, qk), 1) < p, -1.0, 1.0)
    b1, b2 = b1_ref[...], b2_ref[...]
    crw, ciw = wide(cr), sgn_row * wide(ci)
    bb = crw * b1 + ciw * b2
    bb_sw = crw * b2 - ciw * b1
    lane_t = jnp.right_shift(lax.broadcasted_iota(jnp.int32, (PREP_GB, 2 * p, qk), 2), 4)
    pr, pi = jnp.ones_like(ar), jnp.zeros_like(ar)
    pwr = jnp.zeros((PREP_GB, 2 * p, qk), F32)
    pwi = jnp.zeros((PREP_GB, 2 * p, qk), F32)
    for j in range(q):
        sel = lane_t == (q - 1 - j)
        pwr = jnp.where(sel, wide(pr), pwr)
        pwi = jnp.where(sel, wide(pi), pwi)
        if j == q // 2:
            aq8r_ref[...] = pr[:, :p, :]
            aq8i_ref[...] = pi[:, :p, :]
        pr, pi = pr * ar - pi * ai, pr * ai + pi * ar
    binv = pwr * bb + sgn_row * pwi * bb_sw
    bin_ref[...] = binv
    binb_ref[...] = binv.astype(BF16)

    ar, ai = ar_l, ai_l
    low = lax.broadcasted_iota(jnp.int32, (PREP_GB, 1, 2 * p), 2) < p
    sgn_lane = jnp.where(low, 1.0, -1.0)
    cre, cim = cre_ref[...], cim_ref[...]
    pr, pi = jnp.ones_like(ar), jnp.zeros_like(ar)
    for j in range(q + 1):
        p1 = jnp.where(low, pr, pi)
        p2 = jnp.where(low, pi, pr)
        ca_s[:, j * k:(j + 1) * k, :] = sgn_lane * cre * p1 - cim * p2
        if j == q:
            aq16r_ref[...] = pr
            aq16i_ref[...] = pi
        pr, pi = pr * ar - pi * ai, pr * ai + pi * ar
    coff_ref[...] = ca_s[:, k:, :].astype(BF16)

    lane_blk = jnp.right_shift(lax.broadcasted_iota(jnp.int32, (qk, qk), 1), 4)
    for g in range(PREP_GB):
        km = _dot_hi(ca_s[g, :qk, :], bb[g])
        tp = jnp.where(lane_blk == 0, km, 0.0)
        for t in range(1, q):
            sh = jnp.concatenate([jnp.zeros((t * k, qk), F32), km[:qk - t * k, :]], axis=0)
            tp = jnp.where(lane_blk == t, sh, tp)
        toep_ref[g] = tp.astype(BF16)


def _s5_prep(lam_re, lam_im, log_dt, b_re, b_im, c_re, c_im):
    g, p, k, q = S5_GROUPS, S5_STATE, S5_GROUP_SIZE, S5_Q
    qk = q * k
    dup = lambda a: jnp.concatenate([a, a], axis=-1)
    ldt = jnp.broadcast_to(log_dt[:, None], (g, 2 * p))
    rowp = jnp.stack([dup(lam_re), dup(lam_im), ldt], axis=1)
    b1 = jnp.tile(jnp.concatenate([b_re, b_im], axis=1), (1, 1, q))
    b2 = jnp.tile(jnp.concatenate([b_im, b_re], axis=1), (1, 1, q))
    blk = lambda r, c: pl.BlockSpec((PREP_GB, r, c), lambda i: (i, 0, 0))
    shp = lambda r, c, dt: jax.ShapeDtypeStruct((g, r, c), dt)
    toep, coff, bin_, binb, aq16r, aq16i, aq8r, aq8i = pl.pallas_call(
        _s5_prep_body,
        grid=(g // PREP_GB,),
        in_specs=[blk(3, 2 * p), blk(k, 2 * p), blk(k, 2 * p), blk(2 * p, qk), blk(2 * p, qk)],
        out_specs=[blk(qk, qk), blk(qk, 2 * p), blk(2 * p, qk), blk(2 * p, qk),
                   blk(1, 2 * p), blk(1, 2 * p), blk(p, LANES), blk(p, LANES)],
        out_shape=[shp(qk, qk, BF16), shp(qk, 2 * p, BF16), shp(2 * p, qk, F32), shp(2 * p, qk, BF16),
                   shp(1, 2 * p, F32), shp(1, 2 * p, F32), shp(p, LANES, F32), shp(p, LANES, F32)],
        scratch_shapes=[pltpu.VMEM((PREP_GB, (q + 1) * k, 2 * p), F32)],
        compiler_params=_cparams(("arbitrary",)),
        name="s5_prep",
    )(rowp, dup(c_re), dup(c_im), b1, b2)
    return dict(toep=toep, coff=coff, bin=bin_, binb=binb,
                aq={16: (aq16r, aq16i), 8: (aq8r, aq8i)})


def _in_proj_body(x_ref, nw_ref, w_ref, wdt_ref, o_ref, dt_ref, u5_ref, h_ref):
    j = pl.program_id(1)

    @pl.when(j == 0)
    def _():
        x = x_ref[...]
        ms = jnp.mean(x * x, axis=-1, keepdims=True)
        h = (x * lax.rsqrt(ms + NORM_EPS) * nw_ref[...]).astype(BF16)
        h_ref[...] = h
        dt_ref[...] = _dot(h, wdt_ref[...])

    o = _dot(h_ref[...], w_ref[...]).astype(BF16)
    o_ref[...] = o

    @pl.when(j == CB_U5)
    def _():
        u5_ref[...] = o


def _in_proj(x2d, norm_w, w_cat, w_dt):
    m = x2d.shape[0]
    rt = min(ROW_TILE, m)
    return pl.pallas_call(
        _in_proj_body,
        grid=(m // rt, PROJ_COLS // COL_TILE),
        in_specs=[pl.BlockSpec((rt, D_MODEL), lambda i, j: (i, 0)),
                  pl.BlockSpec((1, D_MODEL), lambda i, j: (0, 0)),
                  pl.BlockSpec((D_MODEL, COL_TILE), lambda i, j: (0, j)),
                  pl.BlockSpec((D_MODEL, LANES), lambda i, j: (0, 0))],
        out_specs=[pl.BlockSpec((rt, COL_TILE), lambda i, j: (i, j)),
                   pl.BlockSpec((rt, LANES), lambda i, j: (i, 0)),
                   pl.BlockSpec((rt, COL_TILE), lambda i, j: (i, 0))],
        out_shape=[jax.ShapeDtypeStruct((m, PROJ_COLS), BF16), jax.ShapeDtypeStruct((m, LANES), F32),
                   jax.ShapeDtypeStruct((m, S5_WIDTH), BF16)],
        scratch_shapes=[pltpu.VMEM((rt, D_MODEL), BF16)],
        compiler_params=_cparams(("arbitrary", "arbitrary")),
        name="in_proj",
    )(x2d, norm_w, w_cat, w_dt)


CONV_PAD = SUBLANES
CONV_SEC = 256


def _seg_cumsum_rows(x, seg):
    t = lax.broadcasted_iota(jnp.int32, x.shape, 0) & (seg - 1)
    s = 1
    while s < seg:
        x = x + jnp.where(t >= s, pltpu.roll(x, s, axis=0), 0.0)
        s *= 2
    return x


def _seg_revsum_rows(x, seg):
    n = x.shape[0]
    t = lax.broadcasted_iota(jnp.int32, x.shape, 0) & (seg - 1)
    s = 1
    while s < seg:
        x = x + jnp.where(t + s < seg, pltpu.roll(x, n - s, axis=0), 0.0)
        s *= 2
    return x


def _to_channel_major(src, dst):
    for j in range(SSD_WIDTH // LANES):
        sl = slice(j * LANES, (j + 1) * LANES)
        dst[sl, :] = src[:, sl].astype(F32).T


def _store_time_major(yt, y_ref):
    for j in range(SSD_WIDTH // LANES):
        sl = slice(j * LANES, (j + 1) * LANES)
        y_ref[:, sl] = yt[sl, :].T.astype(y_ref.dtype)


def _group_bc(act, g):
    bg = act[:, SSD_WIDTH + g * SSD_STATE:SSD_WIDTH + (g + 1) * SSD_STATE].astype(BF16)
    cg = act[:, SSD_WIDTH + 1024 + g * SSD_STATE:SSD_WIDTH + 1024 + (g + 1) * SSD_STATE].astype(BF16)
    return bg, cg


SSD_CPS = 4


def _ssd_body(d_ref, xs_ref, b_ref, c_ref, dt_ref, cw_ref, cb_ref, dtb_ref, alog_ref,
              y_ref, ssmo_ref, cbuf, act, xt, yt, state):
    q = SSD_CHUNK
    hp = SSD_HEAD_DIM

    k1 = SSD_CONV - 1
    base = CONV_PAD

    @pl.when(pl.program_id(1) == 0)
    def _():
        state[...] = jnp.zeros_like(state)
        cbuf[:base, :] = jnp.zeros((base, SSD_CONV_DIM), F32)

    causal = lax.broadcasted_iota(jnp.int32, (q, q), 1) >= lax.broadcasted_iota(jnp.int32, (q, q), 0)

    def chunk(ci, carry):
        rows = pl.ds(pl.multiple_of(ci * q, q), q)
        cbuf[base:, 0:SSD_WIDTH] = xs_ref[rows, :].astype(F32)
        cbuf[base:, SSD_WIDTH:SSD_WIDTH + 1024] = b_ref[rows, :].astype(F32)
        cbuf[base:, SSD_WIDTH + 1024:] = c_ref[rows, :].astype(F32)
        for s0 in range(0, SSD_CONV_DIM, CONV_SEC):
            cs = slice(s0, s0 + CONV_SEC)
            full = cbuf[:, cs]
            acc = cb_ref[:, cs] + cw_ref[k1:k1 + 1, cs] * full[base:, :]
            for d in range(1, SSD_CONV):
                acc = acc + cw_ref[k1 - d:k1 - d + 1, cs] * pltpu.roll(full, d, axis=0)[base:, :]
            act[:, cs] = _silu(acc)
        cbuf[base - k1:base, :] = cbuf[base + q - k1:base + q, :]

        dt = jax.nn.softplus(dt_ref[rows, :] + dtb_ref[...])
        a_cs = _seg_cumsum_rows(dt * (-jnp.exp(alog_ref[...])), q) * LOG2E
        a_cst = a_cs.T
        dtt = dt.T
        _to_channel_major(act, xt)

        for g in range(SSD_GROUPS):
            bg, cg = _group_bc(act, g)
            cbt = _dot_nt(bg, cg)
            stg = state[g * SSD_GW:(g + 1) * SSD_GW, :]
            yoff = _dot_nt(stg.astype(BF16), cg)
            for r in range(SSD_HPG):
                h = g * SSD_HPG + r
                hs = slice(h * hp, (h + 1) * hp)
                arow = a_cst[h:h + 1, :]
                acol = a_cs[:, h:h + 1]
                lt = jnp.where(causal, jnp.exp2(arow - acol), 0.0)
                wt = (cbt * lt).astype(BF16)
                xh = xt[hs, :]
                xdt = xh * dtt[h:h + 1, :]
                yt[hs, :] = (_dot(xdt.astype(BF16), wt) + yoff[r * hp:(r + 1) * hp, :] * jnp.exp2(arow)
                             + d_ref[h] * xh)
                atot = arow[:, q - 1:q]
                xdec = xdt * jnp.exp2(atot - arow)
                state[hs, :] = jnp.exp2(atot) * stg[r * hp:(r + 1) * hp, :] + _dot(xdec.astype(BF16), bg)

        _store_time_major(yt, y_ref.at[rows, :])
        return carry

    lax.fori_loop(0, SSD_CPS, chunk, 0)
    ssmo_ref[...] = state[...]


def _ssd_consts(conv_w, conv_b, dt_bias, a_log):
    padl = lambda v: jnp.pad(v, (0, LANES - SSD_HEADS)).reshape(1, LANES)
    return (conv_w, conv_b.reshape(1, -1), padl(dt_bias), padl(a_log))


def _ssd_const_specs():
    const = lambda shape: pl.BlockSpec(shape, lambda *_: (0, 0))
    return [const((SSD_CONV, SSD_CONV_DIM)), const((1, SSD_CONV_DIM)), const((1, LANES)), const((1, LANES))]


def _ssd_proj_specs(rows, rowf):
    return [pl.BlockSpec((rows, SSD_WIDTH), lambda *a: (rowf(*a), 1)),
            pl.BlockSpec((rows, 1024), lambda *a: (rowf(*a), CB_B)),
            pl.BlockSpec((rows, 1024), lambda *a: (rowf(*a), CB_C)),
            pl.BlockSpec((rows, LANES), lambda *a: (rowf(*a), 0))]


def _ssd(proj, dt, nseq, nchunk, d_ssd, consts):
    m = proj.shape[0]
    nstep = nchunk // SSD_CPS
    rows = SSD_CPS * SSD_CHUNK
    rowf = lambda b, c: b * nstep + c
    chan = pltpu.VMEM((SSD_WIDTH, LANES), F32)
    return pl.pallas_call(
        _ssd_body,
        grid=(nseq, nstep),
        in_specs=[pl.BlockSpec(memory_space=pltpu.SMEM)] + _ssd_proj_specs(rows, rowf) + _ssd_const_specs(),
        out_specs=[pl.BlockSpec((rows, SSD_WIDTH), lambda b, c: (rowf(b, c), 0)),
                   pl.BlockSpec((None, SSD_WIDTH, SSD_STATE), lambda b, c: (b, 0, 0))],
        out_shape=[jax.ShapeDtypeStruct((m, SSD_WIDTH), BF16),
                   jax.ShapeDtypeStruct((nseq, SSD_WIDTH, SSD_STATE), F32)],
        scratch_shapes=[pltpu.VMEM((CONV_PAD + SSD_CHUNK, SSD_CONV_DIM), F32),
                        pltpu.VMEM((SSD_CHUNK, SSD_CONV_DIM), F32),
                        chan, chan, pltpu.VMEM((SSD_WIDTH, SSD_STATE), F32)],
        compiler_params=_cparams(("arbitrary", "arbitrary")),
        name="ssd",
    )(d_ssd, proj, proj, proj, dt, *consts)


SSD_PACK = 16
SSD_TOK = SSD_CHUNK // SSD_PACK
SSD_SUB = 8


def _ssd_decay_body(dt_ref, dtb_ref, alog_ref, o_ref):
    da = jax.nn.softplus(dt_ref[...] + dtb_ref[...]) * (-jnp.exp(alog_ref[...]))
    o_ref[...] = jnp.exp(jnp.sum(da.reshape(o_ref.shape[0], SSD_TOK, LANES), axis=1))


def _ssd_decay(dt, nseq, dtb, alog):
    const = lambda shape: pl.BlockSpec(shape, lambda i: (0, 0))
    return pl.pallas_call(
        _ssd_decay_body,
        grid=(1,),
        in_specs=[const((nseq * SSD_TOK, LANES)), const((1, LANES)), const((1, LANES))],
        out_specs=const((nseq, LANES)),
        out_shape=jax.ShapeDtypeStruct((nseq, LANES), F32),
        compiler_params=_cparams(("arbitrary",)),
        name="ssd_decay",
    )(dt, dtb, alog)


def _ssd_packed_body(dec_ref, d_ref, xs_ref, b_ref, c_ref, dt_ref, cw_ref, cb_ref, dtb_ref, alog_ref,
                     convp_ref, ssm0_ref,
                     y_ref, ssmo_ref, act, xt, yt, xd, ear, bgs, cgs):
    q = SSD_CHUNK
    tl = SSD_TOK
    hp = SSD_HEAD_DIM
    tile = pl.program_id(0)
    sub = pl.program_id(1)

    @pl.when(sub == 0)
    def _shared():
        rowt = lax.broadcasted_iota(jnp.int32, (q, CONV_SEC), 0) & (tl - 1)
        for s0 in range(0, SSD_CONV_DIM, CONV_SEC):
            cs = slice(s0, s0 + CONV_SEC)
            if s0 < SSD_WIDTH:
                x = xs_ref[:, cs]
            elif s0 < SSD_WIDTH + 1024:
                x = b_ref[:, s0 - SSD_WIDTH:s0 - SSD_WIDTH + CONV_SEC]
            else:
                x = c_ref[:, s0 - SSD_WIDTH - 1024:s0 - SSD_WIDTH - 1024 + CONV_SEC]
            x = x.astype(F32)
            e = convp_ref[:, :, cs].reshape(q, CONV_SEC)
            acc = cb_ref[:, cs] + cw_ref[SSD_CONV - 1:SSD_CONV, cs] * x
            for d in range(1, SSD_CONV):
                prev = jnp.where(rowt >= d, pltpu.roll(x, d, axis=0), pltpu.roll(e, q - tl + d, axis=0))
                acc = acc + cw_ref[SSD_CONV - 1 - d:SSD_CONV - d, cs] * prev
            act[:, cs] = _silu(acc)

        dt = jax.nn.softplus(dt_ref[...] + dtb_ref[...])
        da = dt * (-jnp.exp(alog_ref[...]))
        a_cs = _seg_cumsum_rows(da, tl)
        a_tot = a_cs + _seg_revsum_rows(da, tl) - da
        a_cst, a_tott, dtt = a_cs.T, a_tot.T, dt.T
        ea = jnp.exp(a_cst)
        _to_channel_major(act, xt)

        s_i = lax.broadcasted_iota(jnp.int32, (q, q), 0)
        l_i = lax.broadcasted_iota(jnp.int32, (q, q), 1)
        causal = (l_i >= s_i) & (jnp.right_shift(l_i, 3) == jnp.right_shift(s_i, 3))
        for g in range(SSD_GROUPS):
            bg, cg = _group_bc(act, g)
            bgs[g] = bg
            cgs[g] = cg
            cbt = _dot_nt(bg, cg)
            for r in range(SSD_HPG):
                h = g * SSD_HPG + r
                hs = slice(h * hp, (h + 1) * hp)
                arow = a_cst[h:h + 1, :]
                acol = a_cs[:, h:h + 1]
                lt = jnp.where(causal, jnp.exp(arow - acol), 0.0)
                wt = (cbt * lt).astype(BF16)
                xh = xt[hs, :]
                xdt = xh * dtt[h:h + 1, :]
                yt[hs, :] = _dot(xdt.astype(BF16), wt) + d_ref[h] * xh
                xd[hs, :] = xdt * jnp.exp(a_tott[h:h + 1, :] - arow)
                ear[hs, :] = jnp.broadcast_to(ea[h:h + 1, :], (hp, q))

    lane_seq = jnp.right_shift(lax.broadcasted_iota(jnp.int32, (SSD_GW, q), 1), 3)

    def seq_body(sl, carry):
        sq = sub * SSD_SUB + sl
        inseq = lane_seq == sq
        dbase = (tile * SSD_PACK + sq) * SSD_HEADS
        for g in range(SSD_GROUPS):
            rows = slice(g * SSD_GW, (g + 1) * SSD_GW)
            stg = ssm0_ref[sl, rows, :]
            yo = _dot_nt(stg.astype(BF16), cgs[g])
            yt[rows, :] = yt[rows, :] + jnp.where(inseq, yo * ear[rows, :], 0.0)
            upd = _dot(jnp.where(inseq, xd[rows, :], 0.0).astype(BF16), bgs[g])
            for r in range(SSD_HPG):
                h = g * SSD_HPG + r
                ssmo_ref[sl, h * hp:(h + 1) * hp, :] = (dec_ref[dbase + h] * stg[r * hp:(r + 1) * hp, :]
                                                        + upd[r * hp:(r + 1) * hp, :])
        return carry

    lax.fori_loop(0, SSD_SUB, seq_body, 0)

    @pl.when(sub == SSD_PACK // SSD_SUB - 1)
    def _():
        _store_time_major(yt, y_ref)


def _ssd_packed(proj, dt, nseq, d_ssd, consts, conv0, ssm0):
    m = proj.shape[0]
    ntile = nseq // SSD_PACK
    nsub = SSD_PACK // SSD_SUB
    dec = _ssd_decay(dt, nseq, consts[2], consts[3])[:, :SSD_HEADS].reshape(-1)
    convp = jnp.pad(conv0, ((0, 0), (SSD_TOK - (SSD_CONV - 1), 0), (0, 0)))
    chan = pltpu.VMEM((SSD_WIDTH, LANES), F32)
    st_spec = pl.BlockSpec((SSD_SUB, SSD_WIDTH, SSD_STATE), lambda i, s: (i * nsub + s, 0, 0))
    return pl.pallas_call(
        _ssd_packed_body,
        grid=(ntile, nsub),
        in_specs=[pl.BlockSpec(memory_space=pltpu.SMEM), pl.BlockSpec(memory_space=pltpu.SMEM)]
        + _ssd_proj_specs(SSD_CHUNK, lambda i, s: i) + _ssd_const_specs()
        + [pl.BlockSpec((SSD_PACK, SSD_TOK, SSD_CONV_DIM), lambda i, s: (i, 0, 0)), st_spec],
        out_specs=[pl.BlockSpec((SSD_CHUNK, SSD_WIDTH), lambda i, s: (i, 0)), st_spec],
        out_shape=[jax.ShapeDtypeStruct((m, SSD_WIDTH), BF16),
                   jax.ShapeDtypeStruct((nseq, SSD_WIDTH, SSD_STATE), F32)],
        scratch_shapes=[pltpu.VMEM((SSD_CHUNK, SSD_CONV_DIM), F32), chan, chan, chan, chan,
                        pltpu.VMEM((SSD_GROUPS, SSD_CHUNK, SSD_STATE), BF16),
                        pltpu.VMEM((SSD_GROUPS, SSD_CHUNK, SSD_STATE), BF16)],
        compiler_params=_cparams(("arbitrary", "arbitrary")),
        name="ssd_packed",
    )(dec, d_ssd, proj, proj, proj, dt, *consts, convp, ssm0)


S5_GB = 16
S5_SCAN_SEQS = 4


def _s5_body(*refs, q, scan):
    if scan:
        u_ref, toep_ref, bin_ref, coff_ref, aqr_ref, aqi_ref, y_ref, so_ref, sr_s, si_s = refs
        s0_ref = None
    else:
        u_ref, toep_ref, bin_ref, coff_ref, aqr_ref, aqi_ref, s0_ref, y_ref, so_ref = refs
    k = S5_GROUP_SIZE
    p = S5_STATE
    gp = S5_GB * p
    w = u_ref.shape[2]
    nsq = w // LANES

    def chunk_in(gl):
        return u_ref[:, gl * k:(gl + 1) * k, :].reshape(q * k, w)

    if scan:
        for gl in range(S5_GB):
            bo = _dot(bin_ref[gl], chunk_in(gl).astype(BF16))
            sr_s[gl * p:(gl + 1) * p, :] = bo[:p, :]
            si_s[gl * p:(gl + 1) * p, :] = bo[p:, :]
        row = lax.broadcasted_iota(jnp.int32, (LANES, LANES), 0)
        lane_lo = lax.broadcasted_iota(jnp.int32, (1, LANES), 1) < p

        def shift_down(x, s):
            if s < SUBLANES:
                return jnp.where(row >= s, pltpu.roll(x, s, axis=0), 0.0)
            return jnp.concatenate([jnp.zeros((s, LANES), F32), x[:LANES - s, :]], axis=0)

        for pair in range(S5_GB // 2):
            g0 = 2 * pair
            rs = slice(g0 * p, (g0 + 2) * p)
            a_r = jnp.where(lane_lo, aqr_ref[g0], aqr_ref[g0 + 1])
            a_i = jnp.where(lane_lo, aqi_ref[g0], aqi_ref[g0 + 1])
            for sq in range(nsq):
                ls = slice(sq * LANES, (sq + 1) * LANES)
                xr = sr_s[rs, ls].T
                xi = si_s[rs, ls].T
                pr, pi = a_r, a_i
                s = 1
                while s < LANES:
                    tr, ti = shift_down(xr, s), shift_down(xi, s)
                    xr, xi = xr + (pr * tr - pi * ti), xi + (pr * ti + pi * tr)
                    if 2 * s < LANES:
                        pr, pi = pr * pr - pi * pi, 2.0 * pr * pi
                    s *= 2
                so_ref[sq, pair, 0:1, :] = xr[LANES - 1:LANES, :]
                so_ref[sq, pair, 1:2, :] = xi[LANES - 1:LANES, :]
                sr_s[rs, ls] = shift_down(xr, 1).T
                si_s[rs, ls] = shift_down(xi, 1).T

    for gl in range(S5_GB):
        z = chunk_in(gl)
        if scan:
            sin_r, sin_i = sr_s[gl * p:(gl + 1) * p, :], si_s[gl * p:(gl + 1) * p, :]
        else:
            bo = _dot_hi(bin_ref[gl], z)
            aqr, aqi = aqr_ref[gl], aqi_ref[gl]
            sin_r, sin_i = s0_ref[gl, :p, :], s0_ref[gl, p:, :]
            so_ref[gl, :p, :] = aqr * sin_r - aqi * sin_i + bo[:p, :]
            so_ref[gl, p:, :] = aqr * sin_i + aqi * sin_r + bo[p:, :]
        s_in = jnp.concatenate([sin_r, sin_i], axis=0).astype(BF16)
        y = _dot(toep_ref[gl], z.astype(BF16)) + _dot(coff_ref[gl], s_in)
        y_ref[:, gl * k:(gl + 1) * k, :] = y.reshape(q, k, w).astype(y_ref.dtype)


def _s5(ut, q, scan, prep, s0t=None):
    ctot = ut.shape[2]
    lw = min(S5_SCAN_SEQS * LANES, ctot) if scan else LANES
    nlb = ctot // lw
    ngb = S5_GROUPS // S5_GB
    qk = q * S5_GROUP_SIZE
    gspec = lambda r, c, cb=0: pl.BlockSpec((S5_GB, r, c), lambda gb, lb: (gb, 0, cb))
    in_specs = [pl.BlockSpec((q, S5_GB * S5_GROUP_SIZE, lw), lambda gb, lb: (0, gb, lb)),
                gspec(qk, qk), gspec(2 * S5_STATE, qk, (S5_Q - q) * S5_GROUP_SIZE // qk),
                gspec(qk, 2 * S5_STATE)]
    args = [ut, prep["toep"], prep["binb"] if scan else prep["bin"], prep["coff"], *prep["aq"][q]]
    scratch = []
    if scan:
        in_specs += [gspec(1, 2 * S5_STATE), gspec(1, 2 * S5_STATE)]
        so_shape = jax.ShapeDtypeStruct((ctot // LANES, S5_GROUPS // 2, 2, 2 * S5_STATE), F32)
        so_spec = pl.BlockSpec((lw // LANES, S5_GB // 2, 2, 2 * S5_STATE), lambda gb, lb: (lb, gb, 0, 0))
        scratch = [pltpu.VMEM((S5_GB * S5_STATE, lw), F32)] * 2
    else:
        in_specs += [gspec(S5_STATE, LANES), gspec(S5_STATE, LANES)]
        in_specs.append(gspec(2 * S5_STATE, LANES))
        args.append(s0t)
        so_shape = jax.ShapeDtypeStruct((S5_GROUPS, 2 * S5_STATE, LANES), F32)
        so_spec = gspec(2 * S5_STATE, LANES)
    return pl.pallas_call(
        functools.partial(_s5_body, q=q, scan=scan),
        grid=(ngb, nlb),
        in_specs=in_specs,
        out_specs=[pl.BlockSpec((q, S5_GB * S5_GROUP_SIZE, lw), lambda gb, lb: (0, gb, lb)), so_spec],
        out_shape=[jax.ShapeDtypeStruct(ut.shape, BF16), so_shape],
        scratch_shapes=scratch,
        compiler_params=_cparams(("arbitrary", "arbitrary")),
        name="s5_scan" if scan else "s5_step",
    )(*args)


OUT_ROWS = 512


def _out_body(y5_ref, u5_ref, z5_ref, ga_ref, gb_ref, ys_ref, zs_ref, x_ref, d5_ref, wglu_ref, bglu_ref,
              wd5_ref, nws_ref, wda_ref, wout_ref, fnw_ref, o_ref):
    f32 = lambda r: r[...].astype(F32)
    y5 = jax.nn.gelu(f32(y5_ref) + d5_ref[...] * f32(u5_ref))
    y5 = y5 * _sigmoid(_dot(y5.astype(BF16), wglu_ref[...]) + bglu_ref[...])
    y5 = y5 * _silu(f32(z5_ref))
    y_b = _dot(y5.astype(BF16), wd5_ref[...])
    y_a = None
    for g in range(SSD_GROUPS):
        sl = slice(g * SSD_GW, (g + 1) * SSD_GW)
        v = ys_ref[:, sl].astype(F32) * _silu(zs_ref[:, sl].astype(F32))
        ms = jnp.mean(v * v, axis=-1, keepdims=True)
        yz = (v * lax.rsqrt(ms + NORM_EPS) * nws_ref[:, sl]).astype(BF16)
        part = _dot(yz, wda_ref[sl, :])
        y_a = part if y_a is None else y_a + part
    mix = _sigmoid(f32(ga_ref)) * y_a + _sigmoid(f32(gb_ref)) * y_b
    out = x_ref[...] + _dot(mix.astype(BF16), wout_ref[...])
    ms = jnp.mean(out * out, axis=-1, keepdims=True)
    o_ref[...] = out * lax.rsqrt(ms + NORM_EPS) * fnw_ref[...]


def _out(y5, u5, proj, ys, x2d, prm):
    m = x2d.shape[0]
    rowb = lambda w, cb: pl.BlockSpec((OUT_ROWS, w), lambda i: (i, cb))
    const = lambda r, c: pl.BlockSpec((r, c), lambda i: (0, 0))
    return pl.pallas_call(
        _out_body,
        grid=(m // OUT_ROWS,),
        in_specs=[rowb(1024, 0), rowb(1024, 0), rowb(1024, CB_Z5), rowb(1024, CB_GA), rowb(1024, CB_GB),
                  rowb(SSD_WIDTH, 0), rowb(SSD_WIDTH, 0), rowb(1024, 0),
                  const(1, 1024), const(1024, 1024), const(1, 1024), const(1024, 1024),
                  const(1, SSD_WIDTH), const(SSD_WIDTH, 1024), const(1024, 1024), const(1, 1024)],
        out_specs=rowb(1024, 0),
        out_shape=jax.ShapeDtypeStruct((m, D_MODEL), F32),
        compiler_params=_cparams(("arbitrary",)),
        name="out_stage",
    )(y5, u5, proj, proj, proj, ys, proj, x2d, prm["d_s5"], prm["w_glu"], prm["b_glu"],
      prm["w_down_s5"], prm["ssd_norm_w"], prm["w_down_ssd"], prm["w_out"], prm["final_norm_w"])


def _run(x, conv0, ssm0, re0, im0, prm):
    nseq, seqlen, _ = x.shape
    m = nseq * seqlen
    prompt = conv0 is None
    x2d = x.reshape(m, D_MODEL)
    proj, dt, u5 = _in_proj(x2d, prm["norm_w"], prm["w_cat"], prm["w_dt"])
    conv_new = proj.reshape(nseq, seqlen, PROJ_COLS)[:, seqlen - (SSD_CONV - 1):,
                                                     SSD_WIDTH:SSD_WIDTH + SSD_CONV_DIM].astype(F32)

    if prompt:
        yz, ssm_new = _ssd(proj, dt, nseq, seqlen // SSD_CHUNK, prm["d_ssd"], prm["ssd_consts"])
    else:
        yz, ssm_new = _ssd_packed(proj, dt, nseq, prm["d_ssd"], prm["ssd_consts"], conv0,
                                  ssm0.reshape(nseq, SSD_WIDTH, SSD_STATE))

    q = S5_Q if prompt else seqlen
    ut = jnp.transpose(u5.reshape(m // q, q, S5_WIDTH), (1, 2, 0))
    if prompt:
        yt, so = _s5(ut, q, True, prm["s5"])
        so = so.reshape(nseq, S5_GROUPS // 2, 2, 2, S5_STATE)
        so = jnp.transpose(so, (0, 1, 3, 2, 4)).reshape(nseq, S5_GROUPS, 2, S5_STATE)
        re_new, im_new = so[:, :, 0, :], so[:, :, 1, :]
    else:
        s0t = jnp.transpose(jnp.concatenate([re0, im0], axis=-1), (1, 2, 0))
        yt, so = _s5(ut, q, False, prm["s5"], s0t)
        so = jnp.transpose(so, (2, 0, 1))
        re_new, im_new = so[:, :, :S5_STATE], so[:, :, S5_STATE:]
    y5 = jnp.transpose(yt, (2, 0, 1)).reshape(m, S5_WIDTH)

    out = _out(y5, u5, proj, yz, x2d, prm)
    return (out.reshape(nseq, seqlen, D_MODEL), conv_new[None],
            ssm_new.reshape(1, nseq, SSD_HEADS, SSD_HEAD_DIM, SSD_STATE), re_new[None], im_new[None])


def kernel(x_prompt, x_sample, state_conv, state_ssm, state_s5_re, state_s5_im, norm_w, w_in, conv_w, conv_b,
           dt_bias, A_log, D_ssd, ssd_norm_w, w_down_ssd, lam_re, lam_im, log_dt, B_re, B_im, C_re, C_im, D_s5,
           w_glu, b_glu, w_down_s5, w_out, final_norm_w):
    assert norm_w.shape[0] == 1, "single-layer trunk"
    assert x_prompt.shape[1] % SSD_CHUNK == 0 and x_prompt.shape[1] // S5_Q == LANES
    assert x_sample.shape[1] == SSD_TOK and x_sample.shape[0] == LANES
    w = w_in[0]
    o_dt = SSD_WIDTH + SSD_CONV_DIM
    w_cat = jnp.concatenate([w[:, :o_dt], w[:, o_dt + SSD_HEADS:]], axis=1).astype(BF16)
    w_dt = jnp.pad(w[:, o_dt:o_dt + SSD_HEADS], ((0, 0), (0, LANES - SSD_HEADS))).astype(BF16)
    prm = dict(norm_w=norm_w[0].reshape(1, -1), w_cat=w_cat, w_dt=w_dt,
               ssd_consts=_ssd_consts(conv_w[0], conv_b[0], dt_bias[0], A_log[0]), d_ssd=D_ssd[0],
               ssd_norm_w=ssd_norm_w[0].reshape(1, -1),
               s5=_s5_prep(lam_re[0], lam_im[0], log_dt[0], B_re[0], B_im[0], C_re[0], C_im[0]),
               d_s5=D_s5[0].reshape(1, -1), w_glu=w_glu[0].astype(BF16), b_glu=b_glu[0].reshape(1, -1),
               w_down_s5=w_down_s5[0].astype(BF16), w_down_ssd=w_down_ssd[0].astype(BF16),
               w_out=w_out[0].astype(BF16), final_norm_w=final_norm_w.reshape(1, -1))

    y_p, conv_p, ssm_p, re_p, im_p = _run(x_prompt, None, None, None, None, prm)
    y_s, conv_s, ssm_s, re_s, im_s = _run(x_sample, state_conv[0], state_ssm[0], state_s5_re[0],
                                          state_s5_im[0], prm)
    return (y_p, y_s, conv_p, ssm_p, re_p, im_p, conv_s, ssm_s, re_s, im_s)
```
